```python
import math
import jax, jax.numpy as jnp
from jax import lax
import numpy as np

D_MODEL = 2048
BATCH = 2
SEQ = 16384
DEPTH = 4

GRID_W = 64
CTX_LEN = 256
N_MIXERS = 2
N_CONV_LAYERS = (DEPTH + 1) // 2
N_MLA_LAYERS = DEPTH // 2
N_HEADS = 16
QK_NOPE_DIM = 128
QK_ROPE_DIM = 64
QK_HEAD_DIM = QK_NOPE_DIM + QK_ROPE_DIM
V_HEAD_DIM = 128
Q_LORA_RANK = 512
KV_LORA_RANK = 512
ROPE_THETA = 10000.0
Q_BLOCK = 128
CONV_WIDTH = 3
D_FF = 5632
DN_ALPHA = (2.0 * DEPTH) ** 0.25
DN_BETA = (8.0 * DEPTH) ** -0.25
LN_EPS = 1e-5
RMS_EPS = 1e-6

kernel_name = "hybrid_shortconv_mla_convffn_deepnorm_prefix"


def _layer_norm(x, g, b):
    xf = x.astype(jnp.float32)
    mu = jnp.mean(xf, -1, keepdims=True)
    var = jnp.mean(jnp.square(xf - mu), -1, keepdims=True)
    return ((xf - mu) * lax.rsqrt(var + LN_EPS) * g.astype(jnp.float32) + b.astype(jnp.float32)).astype(x.dtype)


def _rms_norm(x, g):
    xf = x.astype(jnp.float32)
    ms = jnp.mean(jnp.square(xf), -1, keepdims=True)
    return (xf * lax.rsqrt(ms + RMS_EPS) * g.astype(jnp.float32)).astype(x.dtype)


def _dwconv3(h, w, b=None):
    hp = jnp.pad(h, ((0, 0), (1, 1), (0, 0)))
    y = hp[:, :-2] * w[0] + hp[:, 1:-1] * w[1] + hp[:, 2:] * w[2]
    return y if b is None else y + b


def _axial_rope_tables(rows):
    n_freq = QK_ROPE_DIM // 4
    inv_freq = 1.0 / (ROPE_THETA ** (jnp.arange(n_freq, dtype=jnp.float32) / n_freq))
    row = jnp.broadcast_to(jnp.arange(rows, dtype=jnp.float32)[:, None], (rows, GRID_W)).reshape(-1)
    col = jnp.broadcast_to(jnp.arange(GRID_W, dtype=jnp.float32)[None, :], (rows, GRID_W)).reshape(-1)
    ang_r = row[:, None] * inv_freq[None, :]
    ang_c = col[:, None] * inv_freq[None, :]
    ang = jnp.concatenate([ang_r, ang_r, ang_c, ang_c], -1)
    return jnp.cos(ang), jnp.sin(ang)


def _apply_axial_rope(t, cos, sin):
    r1, r2, c1, c2 = jnp.split(t, 4, axis=-1)
    rot = jnp.concatenate([-r2, r1, -c2, c1], -1)
    return (t * cos + rot * sin).astype(t.dtype)


def _short_conv_mix(h, w_in, conv_w, w_out):
    b_gate, c_gate, v = jnp.split(h @ w_in, 3, axis=-1)
    return (b_gate * _dwconv3(c_gate * v, conv_w)) @ w_out


def _mla_queries(h, w_dq, q_norm, w_uq):
    bsz, n, _ = h.shape
    q = (_rms_norm(h @ w_dq, q_norm) @ w_uq).reshape(bsz, n, N_HEADS, QK_HEAD_DIM)
    return q[..., :QK_NOPE_DIM], q[..., QK_NOPE_DIM:]


def _mla_keys_values(h, w_dkv, kv_norm, w_uk, w_uv):
    bsz, n, _ = h.shape
    ckv = h @ w_dkv
    c_kv = _rms_norm(ckv[..., :KV_LORA_RANK], kv_norm)
    k_rope = ckv[..., KV_LORA_RANK:]
    k_nope = (c_kv @ w_uk).reshape(bsz, n, N_HEADS, QK_NOPE_DIM)
    v = (c_kv @ w_uv).reshape(bsz, n, N_HEADS, V_HEAD_DIM)
    return k_nope, k_rope, v


def _block_attention(q_nope, q_rope, k_nope, k_rope, v):
    bsz, lq = q_nope.shape[0], q_nope.shape[1]
    nb = lq // Q_BLOCK
    scale = 1.0 / math.sqrt(QK_HEAD_DIM)
    qn = q_nope.reshape(bsz, nb, Q_BLOCK, N_HEADS, QK_NOPE_DIM).transpose(1, 0, 2, 3, 4)
    qr = q_rope.reshape(bsz, nb, Q_BLOCK, N_HEADS, QK_ROPE_DIM).transpose(1, 0, 2, 3, 4)

    def one_block(args):
        qn_b, qr_b = args
        s = jnp.einsum('bqhd,bkhd->bhqk', qn_b, k_nope) + jnp.einsum('bqhr,bkr->bhqk', qr_b, k_rope)
        p = jax.nn.softmax(s.astype(jnp.float32) * scale, axis=-1).astype(v.dtype)
        return jnp.einsum('bhqk,bkhd->bqhd', p, v)

    o = lax.map(one_block, (qn, qr))
    return o.transpose(1, 0, 2, 3, 4).reshape(bsz, lq, N_HEADS * V_HEAD_DIM)


def _mla_mix(hx, hc, cos, sin, w_dq, q_norm, w_uq, w_dkv, kv_norm, w_uk, w_uv, w_o, need_ctx_out):
    qn_x, qr_x = _mla_queries(hx, w_dq, q_norm, w_uq)
    kn_x, kr_x, v_x = _mla_keys_values(hx, w_dkv, kv_norm, w_uk, w_uv)
    qr_x = _apply_axial_rope(qr_x, cos[:, None, :], sin[:, None, :])
    kr_x = _apply_axial_rope(kr_x, cos, sin)
    kn_c, kr_c, v_c = _mla_keys_values(hc, w_dkv, kv_norm, w_uk, w_uv)
    kn_all = jnp.concatenate([kn_c, kn_x], axis=1)
    kr_all = jnp.concatenate([kr_c, kr_x], axis=1)
    v_all = jnp.concatenate([v_c, v_x], axis=1)
    y_x = _block_attention(qn_x, qr_x, kn_all, kr_all, v_all) @ w_o
    y_c = None
    if need_ctx_out:
        qn_c, qr_c = _mla_queries(hc, w_dq, q_norm, w_uq)
        y_c = _block_attention(qn_c, qr_c, kn_c, kr_c, v_c) @ w_o
    return y_x, y_c


def _conv_ffn(h, w_up, conv_w, conv_b, w_down):
    gu = _dwconv3(h @ w_up, conv_w, conv_b)
    g, u = jnp.split(gu, 2, axis=-1)
    return (jax.nn.silu(g) * u) @ w_down


def setup_inputs(seed: int = 0) -> dict:
    key = jax.random.key(seed)
    ks = jax.random.split(key, 26)
    f32 = jnp.float32
    D, F = D_MODEL, D_FF
    nrm = lambda k, shape, s: jax.random.normal(k, shape, f32) * s
    return {
        "x": nrm(ks[0], (BATCH, SEQ, D), 1.0),
        "c": nrm(ks[1], (BATCH, D), 1.0),
        "ctx": nrm(ks[2], (BATCH, CTX_LEN, D), 1.0),
        "c_ctx": nrm(ks[3], (D,), 1.0),
        "ada_w": nrm(ks[4], (DEPTH, D, 6 * D), 0.5 * D ** -0.5),
        "ada_b": nrm(ks[5], (DEPTH, 6 * D), 0.01),
        "ln1_g": 1.0 + nrm(ks[6], (DEPTH, D), 0.02),
        "ln1_b": nrm(ks[7], (DEPTH, D), 0.02),
        "ln2_g": 1.0 + nrm(ks[8], (DEPTH, D), 0.02),
        "ln2_b": nrm(ks[9], (DEPTH, D), 0.02),
        "sc_w_in": nrm(ks[10], (N_CONV_LAYERS, D, 3 * D), D ** -0.5),
        "sc_conv_w": nrm(ks[11], (N_CONV_LAYERS, CONV_WIDTH, D), CONV_WIDTH ** -0.5),
        "sc_w_out": nrm(ks[12], (N_CONV_LAYERS, D, D), DN_BETA * D ** -0.5),
        "mla_w_dq": nrm(ks[13], (N_MLA_LAYERS, D, Q_LORA_RANK), D ** -0.5),
        "mla_q_norm": 1.0 + nrm(ks[14], (N_MLA_LAYERS, Q_LORA_RANK), 0.02),
        "mla_w_uq": nrm(ks[15], (N_MLA_LAYERS, Q_LORA_RANK, N_HEADS * QK_HEAD_DIM), Q_LORA_RANK ** -0.5),
        "mla_w_dkv": nrm(ks[16], (N_MLA_LAYERS, D, KV_LORA_RANK + QK_ROPE_DIM), D ** -0.5),
        "mla_kv_norm": 1.0 + nrm(ks[17], (N_MLA_LAYERS, KV_LORA_RANK), 0.02),
        "mla_w_uk": nrm(ks[18], (N_MLA_LAYERS, KV_LORA_RANK, N_HEADS * QK_NOPE_DIM), KV_LORA_RANK ** -0.5),
        "mla_w_uv": nrm(ks[19], (N_MLA_LAYERS, KV_LORA_RANK, N_HEADS * V_HEAD_DIM), KV_LORA_RANK ** -0.5),
        "mla_w_o": nrm(ks[20], (N_MLA_LAYERS, N_HEADS * V_HEAD_DIM, D), DN_BETA * (N_HEADS * V_HEAD_DIM) ** -0.5),
        "ffn_w_up": nrm(ks[21], (DEPTH, D, 2 * F), D ** -0.5),
        "ffn_conv_w": nrm(ks[22], (DEPTH, CONV_WIDTH, 2 * F), CONV_WIDTH ** -0.5),
        "ffn_conv_b": nrm(ks[23], (DEPTH, 2 * F), 0.01),
        "ffn_w_down": nrm(ks[24], (DEPTH, F, D), DN_BETA * F ** -0.5),
    }


def reference(x, c, ctx, c_ctx, ada_w, ada_b, ln1_g, ln1_b, ln2_g, ln2_b,
              sc_w_in, sc_conv_w, sc_w_out,
              mla_w_dq, mla_q_norm, mla_w_uq, mla_w_dkv, mla_kv_norm, mla_w_uk, mla_w_uv, mla_w_o,
              ffn_w_up, ffn_conv_w, ffn_conv_b, ffn_w_down):
    n_lat = x.shape[1]
    rows = n_lat // GRID_W
    cos, sin = _axial_rope_tables(rows)
    silu_c = jax.nn.silu(c)
    silu_cc = jax.nn.silu(c_ctx)
    for i in range(DEPTH):
        last = i == DEPTH - 1
        j = i // N_MIXERS
        mod_x = (silu_c @ ada_w[i] + ada_b[i])[:, None, :]
        mod_c = (silu_cc @ ada_w[i] + ada_b[i])[None, None, :]
        sh1, sc1, g1, sh2, sc2, g2 = jnp.split(mod_x, 6, axis=-1)
        csh1, csc1, cg1, csh2, csc2, cg2 = jnp.split(mod_c, 6, axis=-1)
        hx = x * (1.0 + sc1) + sh1
        hc = ctx * (1.0 + csc1) + csh1
        if i % N_MIXERS == 0:
            yx = _short_conv_mix(hx, sc_w_in[j], sc_conv_w[j], sc_w_out[j])
            yc = None if last else _short_conv_mix(hc, sc_w_in[j], sc_conv_w[j], sc_w_out[j])
        else:
            yx, yc = _mla_mix(hx, hc, cos, sin, mla_w_dq[j], mla_q_norm[j], mla_w_uq[j],
                              mla_w_dkv[j], mla_kv_norm[j], mla_w_uk[j], mla_w_uv[j], mla_w_o[j],
                              not last)
        x = _layer_norm(DN_ALPHA * x + g1 * yx, ln1_g[i], ln1_b[i])
        fx = _conv_ffn(x * (1.0 + sc2) + sh2, ffn_w_up[i], ffn_conv_w[i], ffn_conv_b[i], ffn_w_down[i])
        x = _layer_norm(DN_ALPHA * x + g2 * fx, ln2_g[i], ln2_b[i])
        if not last:
            ctx = _layer_norm(DN_ALPHA * ctx + cg1 * yc, ln1_g[i], ln1_b[i])
            fc = _conv_ffn(ctx * (1.0 + csc2) + csh2, ffn_w_up[i], ffn_conv_w[i], ffn_conv_b[i], ffn_w_down[i])
            ctx = _layer_norm(DN_ALPHA * ctx + cg2 * fc, ln2_g[i], ln2_b[i])
    return x
```

```python
import functools
import math

import jax
import jax.numpy as jnp
from jax import lax
from jax.experimental import pallas as pl
from jax.experimental.pallas import tpu as pltpu

F32 = jnp.float32
BF16 = jnp.bfloat16

GRID_W = 64
N_HEADS = 16
QK_NOPE_DIM = 128
QK_ROPE_DIM = 64
QK_HEAD_DIM = QK_NOPE_DIM + QK_ROPE_DIM
V_HEAD_DIM = 128
ROPE_THETA = 10000.0
LN_EPS = 1e-5
RMS_EPS = 1e-6

HALO = 16
VMEM_LIMIT = 56 * 1024 * 1024


def _params(*sem):
    return pltpu.CompilerParams(dimension_semantics=sem, vmem_limit_bytes=VMEM_LIMIT)


def _ada_kernel(c_ref, w_ref, b_ref, o_ref):
    c = c_ref[...]
    s = c * jax.nn.sigmoid(c)
    o_ref[0] = jnp.dot(s, w_ref[0], precision=lax.Precision.HIGHEST,
                       preferred_element_type=F32) + b_ref[0]


def _ada_call(crow, ada_w, ada_b):
    depth, d, n6 = ada_w.shape
    tn = math.gcd(n6, 1024)
    return pl.pallas_call(
        _ada_kernel,
        out_shape=jax.ShapeDtypeStruct((depth, 8, n6), F32),
        grid=(depth, n6 // tn),
        in_specs=[
            pl.BlockSpec((8, d), lambda l, j: (0, 0)),
            pl.BlockSpec((1, d, tn), lambda l, j: (l, 0, j)),
            pl.BlockSpec((1, 1, tn), lambda l, j: (l, 0, j)),
        ],
        out_specs=pl.BlockSpec((1, 8, tn), lambda l, j: (l, 0, j)),
        compiler_params=_params("arbitrary", "arbitrary"),
        name="ada_mod",
    )(crow, ada_w, ada_b.reshape(depth, 1, n6))


def _fill_lhs(x_ref, xp_ref, xn_ref, sc_ref, sh_ref, lhs_ref, tm):
    i = pl.program_id(1)
    last = pl.num_programs(1) - 1
    scale = 1.0 + sc_ref[0]
    shift = sh_ref[0]
    lhs_ref[0:tm, :] = (x_ref[0] * scale + shift).astype(BF16)
    row = lax.broadcasted_iota(jnp.int32, (HALO, 1), 0)
    hn = xn_ref[0] * scale + shift
    hp = xp_ref[0] * scale + shift
    row_n = jnp.where(i < last, 0, -1)
    row_p = jnp.where(i > 0, HALO - 1, -1)
    halo = jnp.where(row == row_n, hn, jnp.where(row == row_p, hp, 0.0))
    lhs_ref[tm:tm + HALO, :] = halo.astype(BF16)


def _conv3(p, cw, tm):
    rows = p.shape[0]
    prev = pltpu.roll(p, 1, 0)[0:tm]
    nxt = pltpu.roll(p, rows - 1, 0)[0:tm]
    return cw[0:1] * prev + cw[1:2] * p[0:tm] + cw[2:3] * nxt


def _mixer_up_kernel(x_ref, xp_ref, xn_ref, sc_ref, sh_ref, wb_ref, wc_ref, wv_ref,
                     cw_ref, o_ref, lhs_ref, *, tm):
    @pl.when(pl.program_id(2) == 0)
    def _():
        _fill_lhs(x_ref, xp_ref, xn_ref, sc_ref, sh_ref, lhs_ref, tm)

    lhs = lhs_ref[...]
    pb = jnp.dot(lhs_ref[0:tm, :], wb_ref[...], preferred_element_type=F32)
    pc = jnp.dot(lhs, wc_ref[...], preferred_element_type=F32)
    pv = jnp.dot(lhs, wv_ref[...], preferred_element_type=F32)
    o_ref[0] = (pb * _conv3(pc * pv, cw_ref[...], tm)).astype(BF16)


def _ffn_up_kernel(x_ref, xp_ref, xn_ref, sc_ref, sh_ref, wg_ref, wu_ref,
                   cwg_ref, cwu_ref, cbg_ref, cbu_ref, o_ref, lhs_ref, *, tm):
    @pl.when(pl.program_id(2) == 0)
    def _():
        _fill_lhs(x_ref, xp_ref, xn_ref, sc_ref, sh_ref, lhs_ref, tm)

    lhs = lhs_ref[...]
    g = _conv3(jnp.dot(lhs, wg_ref[...], preferred_element_type=F32), cwg_ref[...], tm) + cbg_ref[...]
    u = _conv3(jnp.dot(lhs, wu_ref[...], preferred_element_type=F32), cwu_ref[...], tm) + cbu_ref[...]
    o_ref[0] = (g * jax.nn.sigmoid(g) * u).astype(BF16)


def _halo_specs(n, d, tm):
    r = tm // HALO
    nblk = n // HALO
    return [
        pl.BlockSpec((1, tm, d), lambda b, i, j: (b, i, 0)),
        pl.BlockSpec((1, HALO, d), lambda b, i, j: (b, jnp.maximum(i * r - 1, 0), 0)),
        pl.BlockSpec((1, HALO, d), lambda b, i, j: (b, jnp.minimum((i + 1) * r, nblk - 1), 0)),
        pl.BlockSpec((1, 1, d), lambda b, i, j: (b, 0, 0)),
        pl.BlockSpec((1, 1, d), lambda b, i, j: (b, 0, 0)),
    ]


def _mixer_up_call(x, sc, sh, w_in, conv_w):
    bsz, n, d = x.shape
    tm = min(512, n)
    fc = math.gcd(d, 512)
    nb = d // fc
    return pl.pallas_call(
        functools.partial(_mixer_up_kernel, tm=tm),
        out_shape=jax.ShapeDtypeStruct((bsz, n, d), BF16),
        grid=(bsz, n // tm, nb),
        in_specs=_halo_specs(n, d, tm) + [
            pl.BlockSpec((d, fc), lambda b, i, j: (0, j)),
            pl.BlockSpec((d, fc), lambda b, i, j: (0, nb + j)),
            pl.BlockSpec((d, fc), lambda b, i, j: (0, 2 * nb + j)),
            pl.BlockSpec((3, fc), lambda b, i, j: (0, j)),
        ],
        out_specs=pl.BlockSpec((1, tm, fc), lambda b, i, j: (b, i, j)),
        scratch_shapes=[pltpu.VMEM((tm + HALO, d), BF16)],
        compiler_params=_params("arbitrary", "arbitrary", "arbitrary"),
        name="mixer_up",
    )(x, x, x, sc, sh, w_in, w_in, w_in, conv_w)


def _ffn_up_call(x, sc, sh, w_up, conv_w, conv_b):
    bsz, n, d = x.shape
    f = w_up.shape[1] // 2
    tm = min(512, n)
    fc = math.gcd(f, 512)
    nb = f // fc
    conv_b = conv_b.reshape(1, 2 * f)
    return pl.pallas_call(
        functools.partial(_ffn_up_kernel, tm=tm),
        out_shape=jax.ShapeDtypeStruct((bsz, n, f), BF16),
        grid=(bsz, n // tm, nb),
        in_specs=_halo_specs(n, d, tm) + [
            pl.BlockSpec((d, fc), lambda b, i, j: (0, j)),
            pl.BlockSpec((d, fc), lambda b, i, j: (0, nb + j)),
            pl.BlockSpec((3, fc), lambda b, i, j: (0, j)),
            pl.BlockSpec((3, fc), lambda b, i, j: (0, nb + j)),
            pl.BlockSpec((1, fc), lambda b, i, j: (0, j)),
            pl.BlockSpec((1, fc), lambda b, i, j: (0, nb + j)),
        ],
        out_specs=pl.BlockSpec((1, tm, fc), lambda b, i, j: (b, i, j)),
        scratch_shapes=[pltpu.VMEM((tm + HALO, d), BF16)],
        compiler_params=_params("arbitrary", "arbitrary", "arbitrary"),
        name="ffn_up",
    )(x, x, x, sc, sh, w_up, w_up, conv_w, conv_w, conv_b, conv_b)


def _proj_ln_kernel(a_ref, w_ref, x_ref, gate_ref, g_ref, b_ref, o_ref, *scratch, nk, alpha):
    def finalize(y):
        r = alpha * x_ref[0] + gate_ref[0] * y
        mu = jnp.mean(r, axis=-1, keepdims=True)
        rc = r - mu
        var = jnp.mean(rc * rc, axis=-1, keepdims=True)
        o_ref[0] = rc * lax.rsqrt(var + LN_EPS) * g_ref[...] + b_ref[...]

    part = jnp.dot(a_ref[0], w_ref[...], preferred_element_type=F32)
    if nk == 1:
        finalize(part)
    else:
        acc_ref, = scratch
        k = pl.program_id(2)

        @pl.when(k == 0)
        def _():
            acc_ref[...] = part

        @pl.when(jnp.logical_and(k > 0, k < nk - 1))
        def _():
            acc_ref[...] += part

        @pl.when(k == nk - 1)
        def _():
            finalize(acc_ref[...] + part)


def _k_chunk(k):
    if k <= 2048:
        return k
    best = 128
    for c in range(128, 2049, 128):
        if k % c == 0:
            best = c
    return best


def _proj_ln_call(a, w, x, gate, g, b, alpha):
    bsz, n, d = x.shape
    kdim = a.shape[-1]
    tm = min(512, n)
    kc = _k_chunk(kdim)
    nk = kdim // kc
    return pl.pallas_call(
        functools.partial(_proj_ln_kernel, nk=nk, alpha=alpha),
        out_shape=jax.ShapeDtypeStruct((bsz, n, d), F32),
        grid=(bsz, n // tm, nk),
        in_specs=[
            pl.BlockSpec((1, tm, kc), lambda b_, i, k: (b_, i, k)),
            pl.BlockSpec((kc, d), lambda b_, i, k: (k, 0)),
            pl.BlockSpec((1, tm, d), lambda b_, i, k: (b_, i, 0)),
            pl.BlockSpec((1, 1, d), lambda b_, i, k: (b_, 0, 0)),
            pl.BlockSpec((1, d), lambda b_, i, k: (0, 0)),
            pl.BlockSpec((1, d), lambda b_, i, k: (0, 0)),
        ],
        out_specs=pl.BlockSpec((1, tm, d), lambda b_, i, k: (b_, i, 0)),
        scratch_shapes=[pltpu.VMEM((tm, d), F32)] if nk > 1 else [],
        compiler_params=_params("arbitrary", "arbitrary", "arbitrary"),
        name="proj_ln",
    )(a, w, x, gate, g.reshape(1, d), b.reshape(1, d))


def _rms(v, g):
    ms = jnp.mean(v * v, axis=-1, keepdims=True)
    return v * lax.rsqrt(ms + RMS_EPS) * g


def _mla_down_kernel(x_ref, sc_ref, sh_ref, w_ref, qn_ref, kvn_ref, cos_ref, sin_ref,
                     ql_ref, ckv_ref, kr_ref, *, rq, rkv):
    h = (x_ref[0] * (1.0 + sc_ref[0]) + sh_ref[0]).astype(BF16)
    r = jnp.dot(h, w_ref[...], preferred_element_type=F32)
    ql_ref[0] = _rms(r[:, 0:rq], qn_ref[...]).astype(BF16)
    ckv_ref[0] = _rms(r[:, rq:rq + rkv], kvn_ref[...]).astype(BF16)
    t = r[:, rq + rkv:rq + rkv + 2 * QK_ROPE_DIM]
    t_swapped = pltpu.roll(t, QK_ROPE_DIM, 1)
    kr_ref[0] = (t * cos_ref[...] + t_swapped * sin_ref[...]).astype(BF16)


def _mla_down_call(x, sc, sh, w, q_norm, kv_norm, cos128, sin128):
    bsz, n, d = x.shape
    rq, rkv = q_norm.shape[-1], kv_norm.shape[-1]
    wn = w.shape[1]
    tm = min(512, n)
    return pl.pallas_call(
        functools.partial(_mla_down_kernel, rq=rq, rkv=rkv),
        out_shape=(jax.ShapeDtypeStruct((bsz, n, rq), BF16),
                   jax.ShapeDtypeStruct((bsz, n, rkv), BF16),
                   jax.ShapeDtypeStruct((bsz, n, 2 * QK_ROPE_DIM), BF16)),
        grid=(bsz, n // tm),
        in_specs=[
            pl.BlockSpec((1, tm, d), lambda b, i: (b, i, 0)),
            pl.BlockSpec((1, 1, d), lambda b, i: (b, 0, 0)),
            pl.BlockSpec((1, 1, d), lambda b, i: (b, 0, 0)),
            pl.BlockSpec((d, wn), lambda b, i: (0, 0)),
            pl.BlockSpec((1, rq), lambda b, i: (0, 0)),
            pl.BlockSpec((1, rkv), lambda b, i: (0, 0)),
            pl.BlockSpec((tm, 2 * QK_ROPE_DIM), lambda b, i: (i, 0)),
            pl.BlockSpec((tm, 2 * QK_ROPE_DIM), lambda b, i: (i, 0)),
        ],
        out_specs=(pl.BlockSpec((1, tm, rq), lambda b, i: (b, i, 0)),
                   pl.BlockSpec((1, tm, rkv), lambda b, i: (b, i, 0)),
                   pl.BlockSpec((1, tm, 2 * QK_ROPE_DIM), lambda b, i: (b, i, 0))),
        compiler_params=_params("arbitrary", "arbitrary"),
        name="mla_down",
    )(x, sc, sh, w, q_norm.reshape(1, rq), kv_norm.reshape(1, rkv), cos128, sin128)


def _q_up_kernel(ql_ref, w_ref, cos_ref, sin_ref, o_ref, *, n_heads, hg, qscale):
    ql = ql_ref[0]
    cos_t = cos_ref[...]
    sin_t = sin_ref[...]
    rows = QK_HEAD_DIM + QK_ROPE_DIM
    for grp in range(n_heads // hg):
        r = lax.dot_general(w_ref[grp * hg * rows:(grp + 1) * hg * rows, :], ql,
                            (((1,), (1,)), ((), ())), preferred_element_type=F32)
        for hh in range(hg):
            h = grp * hg + hh
            base = hh * rows
            o_ref[0, h, 0:QK_NOPE_DIM, :] = (r[base:base + QK_NOPE_DIM] * qscale).astype(BF16)
            rope = (r[base + QK_NOPE_DIM:base + QK_HEAD_DIM] * cos_t
                    + r[base + QK_HEAD_DIM:base + rows] * sin_t)
            o_ref[0, h, QK_NOPE_DIM:QK_HEAD_DIM, :] = (rope * qscale).astype(BF16)


def _q_up_call(ql, wq_t, cos_t, sin_t, n_heads):
    bsz, n, rq = ql.shape
    tm = min(512, n)
    hg = math.gcd(n_heads, 4)
    qscale = math.log2(math.e) / math.sqrt(QK_HEAD_DIM)
    return pl.pallas_call(
        functools.partial(_q_up_kernel, n_heads=n_heads, hg=hg, qscale=qscale),
        out_shape=jax.ShapeDtypeStruct((bsz, n_heads, QK_HEAD_DIM, n), BF16),
        grid=(bsz, n // tm),
        in_specs=[
            pl.BlockSpec((1, tm, rq), lambda b, i: (b, i, 0)),
            pl.BlockSpec(wq_t.shape, lambda b, i: (0, 0)),
            pl.BlockSpec((QK_ROPE_DIM, tm), lambda b, i: (0, i)),
            pl.BlockSpec((QK_ROPE_DIM, tm), lambda b, i: (0, i)),
        ],
        out_specs=pl.BlockSpec((1, n_heads, QK_HEAD_DIM, tm), lambda b, i: (b, 0, 0, i)),
        compiler_params=_params("arbitrary", "arbitrary"),
        name="q_up",
    )(ql, wq_t, cos_t, sin_t)


def _kv_up_kernel(ckv_ref, kr_ref, wk_ref, wv_ref, k_ref, vt_ref, *, n_heads, tm, tk):
    ckv = ckv_ref[0]
    kr = kr_ref[0][:, 0:QK_ROPE_DIM]
    for p in range(n_heads // 2):
        kn = jnp.dot(ckv, wk_ref[:, p * 2 * QK_NOPE_DIM:(p + 1) * 2 * QK_NOPE_DIM],
                     preferred_element_type=F32)
        for e in range(2):
            k_ref[0, 2 * p + e, :, 0:QK_NOPE_DIM] = (
                kn[:, e * QK_NOPE_DIM:(e + 1) * QK_NOPE_DIM].astype(BF16))
            k_ref[0, 2 * p + e, :, QK_NOPE_DIM:QK_HEAD_DIM] = kr
    vt = lax.dot_general(wv_ref[...], ckv, (((1,), (1,)), ((), ())), preferred_element_type=F32)
    for h in range(n_heads):
        for c in range(tm // tk):
            vt_ref[0, h, c] = vt[h * V_HEAD_DIM:(h + 1) * V_HEAD_DIM, c * tk:(c + 1) * tk].astype(BF16)


def _kv_up_call(ckv, kr, w_uk, wv_t, n_heads, tk):
    bsz, n, rkv = ckv.shape
    tm = min(512, n)
    tk = min(tk, tm)
    return pl.pallas_call(
        functools.partial(_kv_up_kernel, n_heads=n_heads, tm=tm, tk=tk),
        out_shape=(jax.ShapeDtypeStruct((bsz, n_heads, n, QK_HEAD_DIM), BF16),
                   jax.ShapeDtypeStruct((bsz, n_heads, n // tk, V_HEAD_DIM, tk), BF16)),
        grid=(bsz, n // tm),
        in_specs=[
            pl.BlockSpec((1, tm, rkv), lambda b, i: (b, i, 0)),
            pl.BlockSpec((1, tm, 2 * QK_ROPE_DIM), lambda b, i: (b, i, 0)),
            pl.BlockSpec(w_uk.shape, lambda b, i: (0, 0)),
            pl.BlockSpec(wv_t.shape, lambda b, i: (0, 0)),
        ],
        out_specs=(pl.BlockSpec((1, n_heads, tm, QK_HEAD_DIM), lambda b, i: (b, 0, i, 0)),
                   pl.BlockSpec((1, n_heads, tm // tk, V_HEAD_DIM, tk), lambda b, i: (b, 0, i, 0, 0))),
        compiler_params=_params("arbitrary", "arbitrary"),
        name="kv_up",
    )(ckv, kr, w_uk, wv_t)


def _attn_kernel(*refs, tq, qs, tk, n_chunks):
    if n_chunks:
        qt_ref, kc_ref, vtc_ref, k_ref, vt_ref, o_ref, m_ref, l_ref, acc_ref = refs
    else:
        qt_ref, kc_ref, vtc_ref, o_ref, m_ref, l_ref, acc_ref = refs
    nsub = tq // qs

    def step(n, kj, vtj, first):
        s = jnp.dot(kj, qt_ref[0, 0, :, n * qs:(n + 1) * qs], preferred_element_type=F32)
        smax = jnp.max(s, axis=0, keepdims=True)
        if first:
            p = jnp.exp2(s - smax)
            m_ref[n] = smax
            l_ref[n] = jnp.sum(p, axis=0, keepdims=True)
            acc_ref[n] = jnp.dot(vtj, p.astype(BF16), preferred_element_type=F32)
        else:
            m_old = m_ref[n]
            m_new = jnp.maximum(m_old, smax)
            alpha = jnp.exp2(m_old - m_new)
            p = jnp.exp2(s - m_new)
            m_ref[n] = m_new
            l_ref[n] = alpha * l_ref[n] + jnp.sum(p, axis=0, keepdims=True)
            acc_ref[n] = alpha * acc_ref[n] + jnp.dot(vtj, p.astype(BF16), preferred_element_type=F32)

    for n in range(nsub):
        step(n, kc_ref[0, 0], vtc_ref[0, 0, 0], True)

    if n_chunks:
        def body(j, carry):
            kj = k_ref[0, 0, pl.ds(pl.multiple_of(j * tk, tk), tk), :]
            vtj = vt_ref[0, 0, j]
            for n in range(nsub):
                step(n, kj, vtj, False)
            return carry

        lax.fori_loop(0, n_chunks, body, 0)

    for n in range(nsub):
        o = acc_ref[n] / l_ref[n]
        o_ref[0, n * qs:(n + 1) * qs, :] = o.T.astype(BF16)


def _attn_call(qt, k_c, vt_c, k_x=None, vt_x=None):
    bsz, n_heads, _, n = qt.shape
    lc = k_c.shape[2]
    tq = min(1024, n)
    qs = min(256, tq)
    in_specs = [
        pl.BlockSpec((1, 1, QK_HEAD_DIM, tq), lambda b, h, i: (b, h, 0, i)),
        pl.BlockSpec((1, 1, lc, QK_HEAD_DIM), lambda b, h, i: (b, h, 0, 0)),
        pl.BlockSpec((1, 1, 1, V_HEAD_DIM, lc), lambda b, h, i: (b, h, 0, 0, 0)),
    ]
    args = [qt, k_c, vt_c]
    tk, n_chunks = lc, 0
    if k_x is not None:
        nk = k_x.shape[2]
        n_chunks, tk = vt_x.shape[2], vt_x.shape[4]
        in_specs += [
            pl.BlockSpec((1, 1, nk, QK_HEAD_DIM), lambda b, h, i: (b, h, 0, 0)),
            pl.BlockSpec((1, 1, n_chunks, V_HEAD_DIM, tk), lambda b, h, i: (b, h, 0, 0, 0)),
        ]
        args += [k_x, vt_x]
    return pl.pallas_call(
        functools.partial(_attn_kernel, tq=tq, qs=qs, tk=tk, n_chunks=n_chunks),
        out_shape=jax.ShapeDtypeStruct((bsz, n, n_heads * V_HEAD_DIM), BF16),
        grid=(bsz, n_heads, n // tq),
        in_specs=in_specs,
        out_specs=pl.BlockSpec((1, tq, V_HEAD_DIM), lambda b, h, i: (b, i, h)),
        scratch_shapes=[pltpu.VMEM((tq // qs, 1, qs), F32),
                        pltpu.VMEM((tq // qs, 1, qs), F32),
                        pltpu.VMEM((tq // qs, V_HEAD_DIM, qs), F32)],
        compiler_params=_params("arbitrary", "arbitrary", "arbitrary"),
        name="mla_attn" if k_x is not None else "ctx_attn",
    )(*args)


def _rot_cols(w):
    q = QK_ROPE_DIM // 4
    return jnp.concatenate([-w[..., q:2 * q], w[..., 0:q], -w[..., 3 * q:4 * q], w[..., 2 * q:3 * q]], -1)


def _rope_tables(n):
    rows = n // GRID_W
    n_freq = QK_ROPE_DIM // 4
    inv_freq = 1.0 / (ROPE_THETA ** (jnp.arange(n_freq, dtype=F32) / n_freq))
    row = jnp.broadcast_to(jnp.arange(rows, dtype=F32)[:, None], (rows, GRID_W)).reshape(-1)
    col = jnp.broadcast_to(jnp.arange(GRID_W, dtype=F32)[None, :], (rows, GRID_W)).reshape(-1)
    ang_r = row[:, None] * inv_freq[None, :]
    ang_c = col[:, None] * inv_freq[None, :]
    ang = jnp.concatenate([ang_r, ang_r, ang_c, ang_c], -1)
    return jnp.cos(ang), jnp.sin(ang)


def kernel(x, c, ctx, c_ctx, ada_w, ada_b, ln1_g, ln1_b, ln2_g, ln2_b, sc_w_in, sc_conv_w, sc_w_out, mla_w_dq, mla_q_norm, mla_w_uq, mla_w_dkv, mla_kv_norm, mla_w_uk, mla_w_uv, mla_w_o, ffn_w_up, ffn_conv_w, ffn_conv_b, ffn_w_down):
    bsz, n, d = x.shape
    lc = ctx.shape[1]
    depth = ada_w.shape[0]
    n_heads = N_HEADS
    alpha = (2.0 * depth) ** 0.25
    tk = 512

    crow = jnp.concatenate([c, c_ctx[None, :], jnp.zeros((8 - bsz - 1, d), F32)], 0)
    mods = _ada_call(crow, ada_w, ada_b)

    def mod_x(i, k):
        return mods[i, 0:bsz, k * d:(k + 1) * d].reshape(bsz, 1, d)

    def mod_c(i, k):
        return jnp.broadcast_to(mods[i, bsz:bsz + 1, k * d:(k + 1) * d].reshape(1, 1, d), (bsz, 1, d))

    cos, sin = _rope_tables(n)
    zpad = jnp.zeros((n, QK_ROPE_DIM), F32)
    cos128_x = jnp.concatenate([cos, zpad], 1)
    sin128_x = jnp.concatenate([sin, zpad], 1)
    cos_t_x, sin_t_x = cos.T, sin.T
    cos128_c = jnp.concatenate([jnp.ones((lc, QK_ROPE_DIM), F32), jnp.zeros((lc, QK_ROPE_DIM), F32)], 1)
    sin128_c = jnp.zeros((lc, 2 * QK_ROPE_DIM), F32)
    cos_t_c = jnp.ones((QK_ROPE_DIM, lc), F32)
    sin_t_c = jnp.zeros((QK_ROPE_DIM, lc), F32)

    for i in range(depth):
        last = i == depth - 1
        j = i // 2
        sh1, sc1, g1, sh2, sc2, g2 = (mod_x(i, k) for k in range(6))
        csh1, csc1, cg1, csh2, csc2, cg2 = (mod_c(i, k) for k in range(6))
        if i % 2 == 0:
            w_in = sc_w_in[j].astype(BF16)
            w_out = sc_w_out[j].astype(BF16)
            zx = _mixer_up_call(x, sc1, sh1, w_in, sc_conv_w[j])
            x1 = _proj_ln_call(zx, w_out, x, g1, ln1_g[i], ln1_b[i], alpha)
            if not last:
                zc = _mixer_up_call(ctx, csc1, csh1, w_in, sc_conv_w[j])
                ctx1 = _proj_ln_call(zc, w_out, ctx, cg1, ln1_g[i], ln1_b[i], alpha)
        else:
            rq = mla_w_dq.shape[-1]
            rkv = mla_kv_norm.shape[-1]
            w_dkv = mla_w_dkv[j]
            w_kr = w_dkv[:, rkv:]
            w_down = jnp.concatenate([mla_w_dq[j], w_dkv, _rot_cols(w_kr)], 1).astype(BF16)
            wq = mla_w_uq[j].reshape(rq, n_heads, QK_HEAD_DIM)
            wq = jnp.concatenate([wq, _rot_cols(wq[..., QK_NOPE_DIM:])], -1)
            wq_t = wq.reshape(rq, n_heads * (QK_HEAD_DIM + QK_ROPE_DIM)).T.astype(BF16)
            w_uk = mla_w_uk[j].astype(BF16)
            wv_t = mla_w_uv[j].T.astype(BF16)
            w_o = mla_w_o[j].astype(BF16)

            ql_x, ckv_x, kr_x = _mla_down_call(x, sc1, sh1, w_down, mla_q_norm[j], mla_kv_norm[j],
                                               cos128_x, sin128_x)
            ql_c, ckv_c, kr_c = _mla_down_call(ctx, csc1, csh1, w_down, mla_q_norm[j], mla_kv_norm[j],
                                               cos128_c, sin128_c)
            k_x, vt_x = _kv_up_call(ckv_x, kr_x, w_uk, wv_t, n_heads, tk)
            k_c, vt_c = _kv_up_call(ckv_c, kr_c, w_uk, wv_t, n_heads, tk)
            qt_x = _q_up_call(ql_x, wq_t, cos_t_x, sin_t_x, n_heads)
            o_x = _attn_call(qt_x, k_c, vt_c, k_x, vt_x)
            x1 = _proj_ln_call(o_x, w_o, x, g1, ln1_g[i], ln1_b[i], alpha)
            if not last:
                qt_c = _q_up_call(ql_c, wq_t, cos_t_c, sin_t_c, n_heads)
                o_c = _attn_call(qt_c, k_c, vt_c)
                ctx1 = _proj_ln_call(o_c, w_o, ctx, cg1, ln1_g[i], ln1_b[i], alpha)

        w_up = ffn_w_up[i].astype(BF16)
        w_dn = ffn_w_down[i].astype(BF16)
        ax = _ffn_up_call(x1, sc2, sh2, w_up, ffn_conv_w[i], ffn_conv_b[i])
        x = _proj_ln_call(ax, w_dn, x1, g2, ln2_g[i], ln2_b[i], alpha)
        if not last:
            ac = _ffn_up_call(ctx1, csc2, csh2, w_up, ffn_conv_w[i], ffn_conv_b[i])
            ctx = _proj_ln_call(ac, w_dn, ctx1, cg2, ln2_g[i], ln2_b[i], alpha)
    return x
```

```python
import functools
import math

import jax
import jax.numpy as jnp
from jax import lax
from jax.experimental import pallas as pl
from jax.experimental.pallas import tpu as pltpu

F32 = jnp.float32
BF16 = jnp.bfloat16

GRID_W = 64
N_HEADS = 16
QK_NOPE_DIM = 128
QK_ROPE_DIM = 64
QK_HEAD_DIM = QK_NOPE_DIM + QK_ROPE_DIM
V_HEAD_DIM = 128
ROPE_THETA = 10000.0
LN_EPS = 1e-5
RMS_EPS = 1e-6

LANES = 128
HALO = 16
ROW_BLOCK = 32
KEY_SPLIT = 2
VMEM_LIMIT = 56 * 1024 * 1024


def _params(*sem):
    return pltpu.CompilerParams(dimension_semantics=sem, vmem_limit_bytes=VMEM_LIMIT)


def _ada_kernel(c_ref, w_ref, b_ref, o_ref):
    c = c_ref[...]
    s = c * jax.nn.sigmoid(c)
    o_ref[0] = jnp.dot(s, w_ref[0], precision=lax.Precision.HIGHEST,
                       preferred_element_type=F32) + b_ref[0]


def _ada_call(crow, ada_w, ada_b):
    depth, d, n6 = ada_w.shape
    tn = math.gcd(n6, 1024)
    return pl.pallas_call(
        _ada_kernel,
        out_shape=jax.ShapeDtypeStruct((depth, 8, n6), F32),
        grid=(depth, n6 // tn),
        in_specs=[
            pl.BlockSpec((8, d), lambda l, j: (0, 0)),
            pl.BlockSpec((1, d, tn), lambda l, j: (l, 0, j)),
            pl.BlockSpec((1, 1, tn), lambda l, j: (l, 0, j)),
        ],
        out_specs=pl.BlockSpec((1, 8, tn), lambda l, j: (l, 0, j)),
        compiler_params=_params("arbitrary", "arbitrary"),
        name="ada_mod",
    )(crow, ada_w, ada_b.reshape(depth, 1, n6))


def _fill_lhs(x_ref, xp_ref, xn_ref, sc_ref, sh_ref, lhs_ref, tm):
    i = pl.program_id(1)
    last = pl.num_programs(1) - 1
    scale = 1.0 + sc_ref[0]
    shift = sh_ref[0]
    lhs_ref[0:tm, :] = (x_ref[0] * scale + shift).astype(BF16)
    row = lax.broadcasted_iota(jnp.int32, (HALO, 1), 0)
    hn = xn_ref[0] * scale + shift
    hp = xp_ref[0] * scale + shift
    row_n = jnp.where(i < last, 0, -1)
    row_p = jnp.where(i > 0, HALO - 1, -1)
    halo = jnp.where(row == row_n, hn, jnp.where(row == row_p, hp, 0.0))
    lhs_ref[tm:tm + HALO, :] = halo.astype(BF16)


def _store_products(p_ref, r, tm):
    p_ref[HALO:HALO + tm, :] = r[0:tm]
    p_ref[0:HALO, :] = r[tm:tm + HALO]
    p_ref[HALO + tm:2 * HALO + tm, :] = r[tm:tm + HALO]


def _conv3_rows(p_ref, cw, r0, cs):
    prev = p_ref[HALO - 1 + r0:HALO - 1 + r0 + ROW_BLOCK, cs]
    cen = p_ref[HALO + r0:HALO + r0 + ROW_BLOCK, cs]
    nxt = p_ref[HALO + 1 + r0:HALO + 1 + r0 + ROW_BLOCK, cs]
    return cw[0:1] * prev + cw[1:2] * cen + cw[2:3] * nxt


def _mixer_up_kernel(x_ref, xp_ref, xn_ref, sc_ref, sh_ref, wb_ref, wc_ref, wv_ref,
                     cw_ref, o_ref, lhs_ref, pb_ref, pu_ref, *, tm, nb):
    j = pl.program_id(2)

    def matmuls(slot):
        lhs = lhs_ref[...]
        pb_ref[slot] = jnp.dot(lhs_ref[0:tm, :], wb_ref[...], preferred_element_type=F32)
        pc = jnp.dot(lhs, wc_ref[...], preferred_element_type=F32)
        pv = jnp.dot(lhs, wv_ref[...], preferred_element_type=F32)
        _store_products(pu_ref.at[slot], pc * pv, tm)

    def gate(slot):
        fc = o_ref.shape[-1]
        for c0 in range(0, fc, LANES):
            cs = slice(c0, c0 + LANES)
            cw = cw_ref[:, cs]
            for r0 in range(0, tm, ROW_BLOCK):
                rows = slice(r0, r0 + ROW_BLOCK)
                z = pb_ref[slot, rows, cs] * _conv3_rows(pu_ref.at[slot], cw, r0, cs)
                o_ref[0, rows, cs] = z.astype(BF16)

    @pl.when(j == 0)
    def _():
        _fill_lhs(x_ref, xp_ref, xn_ref, sc_ref, sh_ref, lhs_ref, tm)
        matmuls(0)

    for par in range(2):
        @pl.when(jnp.logical_and(jnp.logical_and(j > 0, j < nb), j % 2 == par))
        def _():
            gate(1 - par)
            matmuls(par)

    @pl.when(j == nb)
    def _():
        gate((nb - 1) % 2)


def _ffn_up_kernel(x_ref, xp_ref, xn_ref, sc_ref, sh_ref, wg_ref, wu_ref,
                   cwg_ref, cwu_ref, cbg_ref, cbu_ref, o_ref, lhs_ref, pg_ref, pu_ref, *, tm, nb):
    j = pl.program_id(2)

    def matmuls(slot):
        lhs = lhs_ref[...]
        _store_products(pg_ref.at[slot], jnp.dot(lhs, wg_ref[...], preferred_element_type=F32), tm)
        _store_products(pu_ref.at[slot], jnp.dot(lhs, wu_ref[...], preferred_element_type=F32), tm)

    def gate(slot):
        fc = o_ref.shape[-1]
        for c0 in range(0, fc, LANES):
            cs = slice(c0, c0 + LANES)
            cwg, cwu, cbg, cbu = cwg_ref[:, cs], cwu_ref[:, cs], cbg_ref[:, cs], cbu_ref[:, cs]
            for r0 in range(0, tm, ROW_BLOCK):
                g = _conv3_rows(pg_ref.at[slot], cwg, r0, cs) + cbg
                u = _conv3_rows(pu_ref.at[slot], cwu, r0, cs) + cbu
                o_ref[0, r0:r0 + ROW_BLOCK, cs] = (g * jax.nn.sigmoid(g) * u).astype(BF16)

    @pl.when(j == 0)
    def _():
        _fill_lhs(x_ref, xp_ref, xn_ref, sc_ref, sh_ref, lhs_ref, tm)
        matmuls(0)

    for par in range(2):
        @pl.when(jnp.logical_and(jnp.logical_and(j > 0, j < nb), j % 2 == par))
        def _():
            gate(1 - par)
            matmuls(par)

    @pl.when(j == nb)
    def _():
        gate((nb - 1) % 2)


def _halo_specs(n, d, tm):
    r = tm // HALO
    nblk = n // HALO
    return [
        pl.BlockSpec((1, tm, d), lambda b, i, j: (b, i, 0)),
        pl.BlockSpec((1, HALO, d), lambda b, i, j: (b, jnp.maximum(i * r - 1, 0), 0)),
        pl.BlockSpec((1, HALO, d), lambda b, i, j: (b, jnp.minimum((i + 1) * r, nblk - 1), 0)),
        pl.BlockSpec((1, 1, d), lambda b, i, j: (b, 0, 0)),
        pl.BlockSpec((1, 1, d), lambda b, i, j: (b, 0, 0)),
    ]


def _mixer_up_call(x, sc, sh, w_in, conv_w):
    bsz, n, d = x.shape
    tm = min(512, n)
    fc = math.gcd(d, 512)
    nb = d // fc
    cur = lambda j: jnp.minimum(j, nb - 1)
    prv = lambda j: jnp.maximum(j - 1, 0)
    return pl.pallas_call(
        functools.partial(_mixer_up_kernel, tm=tm, nb=nb),
        out_shape=jax.ShapeDtypeStruct((bsz, n, d), BF16),
        grid=(bsz, n // tm, nb + 1),
        in_specs=_halo_specs(n, d, tm) + [
            pl.BlockSpec((d, fc), lambda b, i, j: (0, cur(j))),
            pl.BlockSpec((d, fc), lambda b, i, j: (0, nb + cur(j))),
            pl.BlockSpec((d, fc), lambda b, i, j: (0, 2 * nb + cur(j))),
            pl.BlockSpec((3, fc), lambda b, i, j: (0, prv(j))),
        ],
        out_specs=pl.BlockSpec((1, tm, fc), lambda b, i, j: (b, i, prv(j))),
        scratch_shapes=[pltpu.VMEM((tm + HALO, d), BF16),
                        pltpu.VMEM((2, tm, fc), F32),
                        pltpu.VMEM((2, tm + 2 * HALO, fc), F32)],
        compiler_params=_params("arbitrary", "arbitrary", "arbitrary"),
        name="mixer_up",
    )(x, x, x, sc, sh, w_in, w_in, w_in, conv_w)


def _ffn_up_call(x, sc, sh, w_up, conv_w, conv_b):
    bsz, n, d = x.shape
    f = w_up.shape[1] // 2
    tm = min(512, n)
    fc = math.gcd(f, 512)
    nb = f // fc
    conv_b = conv_b.reshape(1, 2 * f)
    cur = lambda j: jnp.minimum(j, nb - 1)
    prv = lambda j: jnp.maximum(j - 1, 0)
    return pl.pallas_call(
        functools.partial(_ffn_up_kernel, tm=tm, nb=nb),
        out_shape=jax.ShapeDtypeStruct((bsz, n, f), BF16),
        grid=(bsz, n // tm, nb + 1),
        in_specs=_halo_specs(n, d, tm) + [
            pl.BlockSpec((d, fc), lambda b, i, j: (0, cur(j))),
            pl.BlockSpec((d, fc), lambda b, i, j: (0, nb + cur(j))),
            pl.BlockSpec((3, fc), lambda b, i, j: (0, prv(j))),
            pl.BlockSpec((3, fc), lambda b, i, j: (0, nb + prv(j))),
            pl.BlockSpec((1, fc), lambda b, i, j: (0, prv(j))),
            pl.BlockSpec((1, fc), lambda b, i, j: (0, nb + prv(j))),
        ],
        out_specs=pl.BlockSpec((1, tm, fc), lambda b, i, j: (b, i, prv(j))),
        scratch_shapes=[pltpu.VMEM((tm + HALO, d), BF16),
                        pltpu.VMEM((2, tm + 2 * HALO, fc), F32),
                        pltpu.VMEM((2, tm + 2 * HALO, fc), F32)],
        compiler_params=_params("arbitrary", "arbitrary", "arbitrary"),
        name="ffn_up",
    )(x, x, x, sc, sh, w_up, w_up, conv_w, conv_w, conv_b, conv_b)


def _proj_ln_kernel(a_ref, w_ref, x_ref, gate_ref, g_ref, b_ref, o_ref, *scratch, nk, alpha):
    def finalize(y):
        r = alpha * x_ref[0] + gate_ref[0] * y
        mu = jnp.mean(r, axis=-1, keepdims=True)
        rc = r - mu
        var = jnp.mean(rc * rc, axis=-1, keepdims=True)
        o_ref[0] = rc * lax.rsqrt(var + LN_EPS) * g_ref[...] + b_ref[...]

    part = jnp.dot(a_ref[0], w_ref[...], preferred_element_type=F32)
    if nk == 1:
        finalize(part)
    else:
        acc_ref, = scratch
        k = pl.program_id(2)

        @pl.when(k == 0)
        def _():
            acc_ref[...] = part

        @pl.when(jnp.logical_and(k > 0, k < nk - 1))
        def _():
            acc_ref[...] += part

        @pl.when(k == nk - 1)
        def _():
            finalize(acc_ref[...] + part)


def _k_chunk(k):
    if k <= 2048:
        return k
    best = 128
    for c in range(128, 2049, 128):
        if k % c == 0:
            best = c
    return best


def _proj_ln_call(a, w, x, gate, g, b, alpha):
    bsz, n, d = x.shape
    kdim = a.shape[-1]
    tm = min(512, n)
    kc = _k_chunk(kdim)
    nk = kdim // kc
    return pl.pallas_call(
        functools.partial(_proj_ln_kernel, nk=nk, alpha=alpha),
        out_shape=jax.ShapeDtypeStruct((bsz, n, d), F32),
        grid=(bsz, n // tm, nk),
        in_specs=[
            pl.BlockSpec((1, tm, kc), lambda b_, i, k: (b_, i, k)),
            pl.BlockSpec((kc, d), lambda b_, i, k: (k, 0)),
            pl.BlockSpec((1, tm, d), lambda b_, i, k: (b_, i, 0)),
            pl.BlockSpec((1, 1, d), lambda b_, i, k: (b_, 0, 0)),
            pl.BlockSpec((1, d), lambda b_, i, k: (0, 0)),
            pl.BlockSpec((1, d), lambda b_, i, k: (0, 0)),
        ],
        out_specs=pl.BlockSpec((1, tm, d), lambda b_, i, k: (b_, i, 0)),
        scratch_shapes=[pltpu.VMEM((tm, d), F32)] if nk > 1 else [],
        compiler_params=_params("arbitrary", "arbitrary", "arbitrary"),
        name="proj_ln",
    )(a, w, x, gate, g.reshape(1, d), b.reshape(1, d))


def _rms(v, g):
    ms = jnp.mean(v * v, axis=-1, keepdims=True)
    return v * lax.rsqrt(ms + RMS_EPS) * g


def _rope128(t, cos, sin):
    return t * cos + pltpu.roll(t, QK_ROPE_DIM, 1) * sin


def _mla_down_kernel(x_ref, sc_ref, sh_ref, w_ref, qn_ref, kvn_ref, cos_ref, sin_ref,
                     ql_ref, ckv_ref, kr_ref, *, rq, rkv):
    h = (x_ref[0] * (1.0 + sc_ref[0]) + sh_ref[0]).astype(BF16)
    r = jnp.dot(h, w_ref[...], preferred_element_type=F32)
    ql_ref[0] = _rms(r[:, 0:rq], qn_ref[...]).astype(BF16)
    ckv_ref[0] = _rms(r[:, rq:rq + rkv], kvn_ref[...]).astype(BF16)
    t = r[:, rq + rkv:rq + rkv + 2 * QK_ROPE_DIM]
    kr_ref[0] = _rope128(t, cos_ref[...], sin_ref[...]).astype(BF16)


def _mla_down_call(x, sc, sh, w, q_norm, kv_norm, cos128, sin128):
    bsz, n, d = x.shape
    rq, rkv = q_norm.shape[-1], kv_norm.shape[-1]
    wn = w.shape[1]
    tm = min(512, n)
    return pl.pallas_call(
        functools.partial(_mla_down_kernel, rq=rq, rkv=rkv),
        out_shape=(jax.ShapeDtypeStruct((bsz, n, rq), BF16),
                   jax.ShapeDtypeStruct((bsz, n, rkv), BF16),
                   jax.ShapeDtypeStruct((bsz, n, 2 * QK_ROPE_DIM), BF16)),
        grid=(bsz, n // tm),
        in_specs=[
            pl.BlockSpec((1, tm, d), lambda b, i: (b, i, 0)),
            pl.BlockSpec((1, 1, d), lambda b, i: (b, 0, 0)),
            pl.BlockSpec((1, 1, d), lambda b, i: (b, 0, 0)),
            pl.BlockSpec((d, wn), lambda b, i: (0, 0)),
            pl.BlockSpec((1, rq), lambda b, i: (0, 0)),
            pl.BlockSpec((1, rkv), lambda b, i: (0, 0)),
            pl.BlockSpec((tm, 2 * QK_ROPE_DIM), lambda b, i: (i, 0)),
            pl.BlockSpec((tm, 2 * QK_ROPE_DIM), lambda b, i: (i, 0)),
        ],
        out_specs=(pl.BlockSpec((1, tm, rq), lambda b, i: (b, i, 0)),
                   pl.BlockSpec((1, tm, rkv), lambda b, i: (b, i, 0)),
                   pl.BlockSpec((1, tm, 2 * QK_ROPE_DIM), lambda b, i: (b, i, 0))),
        compiler_params=_params("arbitrary", "arbitrary"),
        name="mla_down",
    )(x, sc, sh, w, q_norm.reshape(1, rq), kv_norm.reshape(1, rkv), cos128, sin128)


def _q_up_kernel(ql_ref, w_ref, cos_ref, sin_ref, o_ref, *, n_heads, qscale):
    ql = ql_ref[0]
    cos = cos_ref[...] * qscale
    sin = sin_ref[...] * qscale
    wh = QK_NOPE_DIM + 2 * QK_ROPE_DIM
    for h in range(n_heads):
        r = jnp.dot(ql, w_ref[:, h * wh:(h + 1) * wh], preferred_element_type=F32)
        o_ref[0, h, :, 0:QK_NOPE_DIM] = (r[:, 0:QK_NOPE_DIM] * qscale).astype(BF16)
        rope = _rope128(r[:, QK_NOPE_DIM:wh], cos, sin)
        o_ref[0, h, :, QK_NOPE_DIM:QK_HEAD_DIM] = rope[:, 0:QK_ROPE_DIM].astype(BF16)


def _q_up_call(ql, wq, cos128, sin128, n_heads):
    bsz, n, rq = ql.shape
    tm = min(512, n)
    qscale = math.log2(math.e) / math.sqrt(QK_HEAD_DIM)
    return pl.pallas_call(
        functools.partial(_q_up_kernel, n_heads=n_heads, qscale=qscale),
        out_shape=jax.ShapeDtypeStruct((bsz, n_heads, n, QK_HEAD_DIM), BF16),
        grid=(bsz, n // tm),
        in_specs=[
            pl.BlockSpec((1, tm, rq), lambda b, i: (b, i, 0)),
            pl.BlockSpec(wq.shape, lambda b, i: (0, 0)),
            pl.BlockSpec((tm, 2 * QK_ROPE_DIM), lambda b, i: (i, 0)),
            pl.BlockSpec((tm, 2 * QK_ROPE_DIM), lambda b, i: (i, 0)),
        ],
        out_specs=pl.BlockSpec((1, n_heads, tm, QK_HEAD_DIM), lambda b, i: (b, 0, i, 0)),
        compiler_params=_params("arbitrary", "arbitrary"),
        name="q_up",
    )(ql, wq, cos128, sin128)


def _kv_up_kernel(ckv_ref, kr_ref, w_ref, k_ref, v_ref, *, n_heads):
    ckv = ckv_ref[0]
    kr = kr_ref[0][:, 0:QK_ROPE_DIM]
    wh = QK_NOPE_DIM + V_HEAD_DIM
    for h in range(n_heads):
        r = jnp.dot(ckv, w_ref[:, h * wh:(h + 1) * wh], preferred_element_type=F32)
        k_ref[0, h, :, 0:QK_NOPE_DIM] = r[:, 0:QK_NOPE_DIM].astype(BF16)
        k_ref[0, h, :, QK_NOPE_DIM:QK_HEAD_DIM] = kr
        v_ref[0, h] = r[:, QK_NOPE_DIM:wh].astype(BF16)


def _kv_up_call(ckv, kr, w_kv, n_heads):
    bsz, n, rkv = ckv.shape
    tm = min(512, n)
    return pl.pallas_call(
        functools.partial(_kv_up_kernel, n_heads=n_heads),
        out_shape=(jax.ShapeDtypeStruct((bsz, n_heads, n, QK_HEAD_DIM), BF16),
                   jax.ShapeDtypeStruct((bsz, n_heads, n, V_HEAD_DIM), BF16)),
        grid=(bsz, n // tm),
        in_specs=[
            pl.BlockSpec((1, tm, rkv), lambda b, i: (b, i, 0)),
            pl.BlockSpec((1, tm, 2 * QK_ROPE_DIM), lambda b, i: (b, i, 0)),
            pl.BlockSpec(w_kv.shape, lambda b, i: (0, 0)),
        ],
        out_specs=(pl.BlockSpec((1, n_heads, tm, QK_HEAD_DIM), lambda b, i: (b, 0, i, 0)),
                   pl.BlockSpec((1, n_heads, tm, V_HEAD_DIM), lambda b, i: (b, 0, i, 0))),
        compiler_params=_params("arbitrary", "arbitrary"),
        name="kv_up",
    )(ckv, kr, w_kv)


_NT_DIMS = (((1,), (1,)), ((), ()))


def _tile_lanes(v, width):
    return pltpu.repeat(v, width // LANES, 1) if width > LANES else v


def _lane_tile_max(s):
    pm = s[:, 0:LANES]
    for c0 in range(LANES, s.shape[1], LANES):
        pm = jnp.maximum(pm, s[:, c0:c0 + LANES])
    return pm


def _ctx_attn_kernel(q_ref, kc_ref, vc_ref, o_ref):
    s = lax.dot_general(q_ref[0, 0], kc_ref[0, 0], _NT_DIMS, preferred_element_type=F32)
    p = jnp.exp2(s - jnp.max(s, axis=1, keepdims=True))
    o = jnp.dot(p.astype(BF16), vc_ref[0, 0], preferred_element_type=F32)
    o_ref[0] = (o / jnp.sum(p, axis=1, keepdims=True)).astype(BF16)


def _attn_kernel(q_ref, kc_ref, vc_ref, k_ref, v_ref, o_ref,
                 m_ref, l_ref, acc_ref, s_ref, pm_ref, p_ref, al_ref, *, tk, n_chunks):
    tq = q_ref.shape[2]

    def scores(j, slot):
        pm = None
        for c0 in range(0, tk, tk // KEY_SPLIT):
            start = pl.multiple_of(j * tk + c0, tk // KEY_SPLIT)
            s = lax.dot_general(q_ref[0, 0], k_ref[0, 0, pl.ds(start, tk // KEY_SPLIT), :], _NT_DIMS,
                                preferred_element_type=F32)
            s_ref[slot, :, c0:c0 + tk // KEY_SPLIT] = s
            pm = _lane_tile_max(s) if pm is None else jnp.maximum(pm, _lane_tile_max(s))
        pm_ref[slot] = pm

    def softmax(slot):
        m_old = m_ref[...]
        m_new = jnp.maximum(m_old, jnp.max(pm_ref[slot], axis=1, keepdims=True))
        m_ref[...] = m_new
        al_ref[slot] = jnp.exp2(m_old - m_new)
        for r0 in range(0, tq, ROW_BLOCK):
            rows = slice(r0, r0 + ROW_BLOCK)
            p = jnp.exp2(s_ref[slot, rows, :] - _tile_lanes(m_ref[rows, :], tk))
            p_ref[slot, rows, :] = p.astype(BF16)
            l_ref[rows, :] = al_ref[slot, rows, :] * l_ref[rows, :] + jnp.sum(p, axis=1, keepdims=True)

    def weighted_values(j, slot):
        start = pl.multiple_of(j * tk, tk)
        pv = jnp.dot(p_ref[slot], v_ref[0, 0, pl.ds(start, tk), :], preferred_element_type=F32)
        acc_ref[...] = al_ref[slot] * acc_ref[...] + pv

    s = lax.dot_general(q_ref[0, 0], kc_ref[0, 0], _NT_DIMS, preferred_element_type=F32)
    m0 = jnp.broadcast_to(jnp.max(s, axis=1, keepdims=True), (tq, LANES))
    p = jnp.exp2(s - _tile_lanes(m0, s.shape[1]))
    m_ref[...] = m0
    l_ref[...] = jnp.broadcast_to(jnp.sum(p, axis=1, keepdims=True), (tq, LANES))
    acc_ref[...] = jnp.dot(p.astype(BF16), vc_ref[0, 0], preferred_element_type=F32)

    scores(0, 0)
    softmax(0)
    scores(1, 1)

    def body(i, carry):
        j = 2 * i + 1
        weighted_values(j - 1, 0)
        scores(j + 1, 0)
        softmax(1)
        weighted_values(j, 1)
        scores(j + 2, 1)
        softmax(0)
        return carry

    lax.fori_loop(0, (n_chunks - 2) // 2, body, 0)
    weighted_values(n_chunks - 2, 0)
    softmax(1)
    weighted_values(n_chunks - 1, 1)
    o_ref[0] = (acc_ref[...] / l_ref[...]).astype(BF16)


def _attn_call(q, k_c, v_c, k_x=None, v_x=None, tk=1024):
    bsz, n_heads, n, _ = q.shape
    lc = k_c.shape[2]
    tq = min(1024, n)
    in_specs = [
        pl.BlockSpec((1, 1, tq, QK_HEAD_DIM), lambda b, h, i: (b, h, i, 0)),
        pl.BlockSpec((1, 1, lc, QK_HEAD_DIM), lambda b, h, i: (b, h, 0, 0)),
        pl.BlockSpec((1, 1, lc, V_HEAD_DIM), lambda b, h, i: (b, h, 0, 0)),
    ]
    out_shape = jax.ShapeDtypeStruct((bsz, n, n_heads * V_HEAD_DIM), BF16)
    out_specs = pl.BlockSpec((1, tq, V_HEAD_DIM), lambda b, h, i: (b, i, h))
    if k_x is None:
        return pl.pallas_call(
            _ctx_attn_kernel, out_shape=out_shape, grid=(bsz, n_heads, n // tq),
            in_specs=in_specs, out_specs=out_specs,
            compiler_params=_params("arbitrary", "arbitrary", "arbitrary"),
            name="ctx_attn",
        )(q, k_c, v_c)
    nk = k_x.shape[2]
    tk = min(tk, nk // 2)
    n_chunks = nk // tk
    assert n_chunks % 2 == 0 and n_chunks * tk == nk
    in_specs += [
        pl.BlockSpec((1, 1, nk, QK_HEAD_DIM), lambda b, h, i: (b, h, 0, 0)),
        pl.BlockSpec((1, 1, nk, V_HEAD_DIM), lambda b, h, i: (b, h, 0, 0)),
    ]
    return pl.pallas_call(
        functools.partial(_attn_kernel, tk=tk, n_chunks=n_chunks),
        out_shape=out_shape,
        grid=(bsz, n_heads, n // tq),
        in_specs=in_specs,
        out_specs=out_specs,
        scratch_shapes=[pltpu.VMEM((tq, LANES), F32),
                        pltpu.VMEM((tq, LANES), F32),
                        pltpu.VMEM((tq, V_HEAD_DIM), F32),
                        pltpu.VMEM((2, tq, tk), F32),
                        pltpu.VMEM((2, tq, LANES), F32),
                        pltpu.VMEM((2, tq, tk), BF16),
                        pltpu.VMEM((2, tq, LANES), F32)],
        compiler_params=_params("arbitrary", "arbitrary", "arbitrary"),
        name="mla_attn",
    )(q, k_c, v_c, k_x, v_x)


def _rot_cols(w):
    q = QK_ROPE_DIM // 4
    return jnp.concatenate([-w[..., q:2 * q], w[..., 0:q], -w[..., 3 * q:4 * q], w[..., 2 * q:3 * q]], -1)


def _rope_tables(n):
    rows = n // GRID_W
    n_freq = QK_ROPE_DIM // 4
    inv_freq = 1.0 / (ROPE_THETA ** (jnp.arange(n_freq, dtype=F32) / n_freq))
    row = jnp.broadcast_to(jnp.arange(rows, dtype=F32)[:, None], (rows, GRID_W)).reshape(-1)
    col = jnp.broadcast_to(jnp.arange(GRID_W, dtype=F32)[None, :], (rows, GRID_W)).reshape(-1)
    ang_r = row[:, None] * inv_freq[None, :]
    ang_c = col[:, None] * inv_freq[None, :]
    ang = jnp.concatenate([ang_r, ang_r, ang_c, ang_c], -1)
    return jnp.cos(ang), jnp.sin(ang)


def kernel(x, c, ctx, c_ctx, ada_w, ada_b, ln1_g, ln1_b, ln2_g, ln2_b, sc_w_in, sc_conv_w, sc_w_out, mla_w_dq, mla_q_norm, mla_w_uq, mla_w_dkv, mla_kv_norm, mla_w_uk, mla_w_uv, mla_w_o, ffn_w_up, ffn_conv_w, ffn_conv_b, ffn_w_down):
    bsz, n, d = x.shape
    lc = ctx.shape[1]
    depth = ada_w.shape[0]
    n_heads = N_HEADS
    alpha = (2.0 * depth) ** 0.25

    crow = jnp.concatenate([c, c_ctx[None, :], jnp.zeros((8 - bsz - 1, d), F32)], 0)
    mods = _ada_call(crow, ada_w, ada_b)

    def mod_x(i, k):
        return mods[i, 0:bsz, k * d:(k + 1) * d].reshape(bsz, 1, d)

    def mod_c(i, k):
        return jnp.broadcast_to(mods[i, bsz:bsz + 1, k * d:(k + 1) * d].reshape(1, 1, d), (bsz, 1, d))

    cos, sin = _rope_tables(n)
    zpad = jnp.zeros((n, QK_ROPE_DIM), F32)
    cos128_x = jnp.concatenate([cos, zpad], 1)
    sin128_x = jnp.concatenate([sin, zpad], 1)
    cos128_c = jnp.concatenate([jnp.ones((lc, QK_ROPE_DIM), F32), jnp.zeros((lc, QK_ROPE_DIM), F32)], 1)
    sin128_c = jnp.zeros((lc, 2 * QK_ROPE_DIM), F32)

    for i in range(depth):
        last = i == depth - 1
        j = i // 2
        sh1, sc1, g1, sh2, sc2, g2 = (mod_x(i, k) for k in range(6))
        csh1, csc1, cg1, csh2, csc2, cg2 = (mod_c(i, k) for k in range(6))
        if i % 2 == 0:
            w_in = sc_w_in[j].astype(BF16)
            w_out = sc_w_out[j].astype(BF16)
            zx = _mixer_up_call(x, sc1, sh1, w_in, sc_conv_w[j])
            x1 = _proj_ln_call(zx, w_out, x, g1, ln1_g[i], ln1_b[i], alpha)
            if not last:
                zc = _mixer_up_call(ctx, csc1, csh1, w_in, sc_conv_w[j])
                ctx1 = _proj_ln_call(zc, w_out, ctx, cg1, ln1_g[i], ln1_b[i], alpha)
        else:
            rq = mla_w_dq.shape[-1]
            rkv = mla_kv_norm.shape[-1]
            w_dkv = mla_w_dkv[j]
            w_down = jnp.concatenate([mla_w_dq[j], w_dkv, _rot_cols(w_dkv[:, rkv:])], 1).astype(BF16)
            wq = mla_w_uq[j].reshape(rq, n_heads, QK_HEAD_DIM)
            wq = jnp.concatenate([wq, _rot_cols(wq[..., QK_NOPE_DIM:])], -1)
            wq = wq.reshape(rq, n_heads * (QK_HEAD_DIM + QK_ROPE_DIM)).astype(BF16)
            w_kv = jnp.concatenate([mla_w_uk[j].reshape(rkv, n_heads, QK_NOPE_DIM),
                                    mla_w_uv[j].reshape(rkv, n_heads, V_HEAD_DIM)], -1)
            w_kv = w_kv.reshape(rkv, n_heads * (QK_NOPE_DIM + V_HEAD_DIM)).astype(BF16)
            w_o = mla_w_o[j].astype(BF16)

            ql_x, ckv_x, kr_x = _mla_down_call(x, sc1, sh1, w_down, mla_q_norm[j], mla_kv_norm[j],
                                               cos128_x, sin128_x)
            ql_c, ckv_c, kr_c = _mla_down_call(ctx, csc1, csh1, w_down, mla_q_norm[j], mla_kv_norm[j],
                                               cos128_c, sin128_c)
            k_x, v_x = _kv_up_call(ckv_x, kr_x, w_kv, n_heads)
            k_c, v_c = _kv_up_call(ckv_c, kr_c, w_kv, n_heads)
            q_x = _q_up_call(ql_x, wq, cos128_x, sin128_x, n_heads)
            o_x = _attn_call(q_x, k_c, v_c, k_x, v_x)
            x1 = _proj_ln_call(o_x, w_o, x, g1, ln1_g[i], ln1_b[i], alpha)
            if not last:
                q_c = _q_up_call(ql_c, wq, cos128_c, sin128_c, n_heads)
                o_c = _attn_call(q_c, k_c, v_c)
                ctx1 = _proj_ln_call(o_c, w_o, ctx, cg1, ln1_g[i], ln1_b[i], alpha)

        w_up = ffn_w_up[i].astype(BF16)
        w_dn = ffn_w_down[i].astype(BF16)
        ax = _ffn_up_call(x1, sc2, sh2, w_up, ffn_conv_w[i], ffn_conv_b[i])
        x = _proj_ln_call(ax, w_dn, x1, g2, ln2_g[i], ln2_b[i], alpha)
        if not last:
            ac = _ffn_up_call(ctx1, csc2, csh2, w_up, ffn_conv_w[i], ffn_conv_b[i])
            ctx = _proj_ln_call(ac, w_dn, ctx1, cg2, ln2_g[i], ln2_b[i], alpha)
    return x
```

```python
import functools
import math

import jax
import jax.numpy as jnp
from jax import lax
from jax.experimental import pallas as pl
from jax.experimental.pallas import tpu as pltpu

F32 = jnp.float32
BF16 = jnp.bfloat16

GRID_W = 64
N_HEADS = 16
QK_NOPE_DIM = 128
QK_ROPE_DIM = 64
QK_HEAD_DIM = QK_NOPE_DIM + QK_ROPE_DIM
V_HEAD_DIM = 128
ROPE_THETA = 10000.0
LN_EPS = 1e-5
RMS_EPS = 1e-6

LANES = 128
HALO = 16
ROW_BLOCK = 32
MAX_K_CHUNK = 2816
VMEM_LIMIT = 56 * 1024 * 1024


def _params(*sem):
    return pltpu.CompilerParams(dimension_semantics=sem, vmem_limit_bytes=VMEM_LIMIT)


def _ada_kernel(c_ref, w_ref, b_ref, o_ref):
    c = c_ref[...]
    s = c * jax.nn.sigmoid(c)
    o_ref[0] = jnp.dot(s, w_ref[0], precision=lax.Precision.HIGHEST,
                       preferred_element_type=F32) + b_ref[0]


def _ada_call(crow, ada_w, ada_b):
    depth, d, n6 = ada_w.shape
    tn = math.gcd(n6, 1024)
    return pl.pallas_call(
        _ada_kernel,
        out_shape=jax.ShapeDtypeStruct((depth, 8, n6), F32),
        grid=(depth, n6 // tn),
        in_specs=[
            pl.BlockSpec((8, d), lambda l, j: (0, 0)),
            pl.BlockSpec((1, d, tn), lambda l, j: (l, 0, j)),
            pl.BlockSpec((1, 1, tn), lambda l, j: (l, 0, j)),
        ],
        out_specs=pl.BlockSpec((1, 8, tn), lambda l, j: (l, 0, j)),
        compiler_params=_params("arbitrary", "arbitrary"),
        name="ada_mod",
    )(crow, ada_w, ada_b.reshape(depth, 1, n6))


def _fill_lhs(x_ref, xp_ref, xn_ref, sc_ref, sh_ref, lhs_ref, tm):
    i = pl.program_id(1)
    last = pl.num_programs(1) - 1
    scale = 1.0 + sc_ref[0]
    shift = sh_ref[0]
    lhs_ref[0:tm, :] = (x_ref[0] * scale + shift).astype(BF16)
    row = lax.broadcasted_iota(jnp.int32, (HALO, 1), 0)
    hn = xn_ref[0] * scale + shift
    hp = xp_ref[0] * scale + shift
    row_n = jnp.where(i < last, 0, -1)
    row_p = jnp.where(i > 0, HALO - 1, -1)
    halo = jnp.where(row == row_n, hn, jnp.where(row == row_p, hp, 0.0))
    lhs_ref[tm:tm + HALO, :] = halo.astype(BF16)


def _conv3(p, cw, tm):
    rows = p.shape[0]
    prev = pltpu.roll(p, 1, 0)[0:tm]
    nxt = pltpu.roll(p, rows - 1, 0)[0:tm]
    return cw[0:1] * prev + cw[1:2] * p[0:tm] + cw[2:3] * nxt


def _mixer_up_kernel(x_ref, xp_ref, xn_ref, sc_ref, sh_ref, wb_ref, wc_ref, wv_ref,
                     cw_ref, o_ref, lhs_ref, *, tm):
    @pl.when(pl.program_id(2) == 0)
    def _():
        _fill_lhs(x_ref, xp_ref, xn_ref, sc_ref, sh_ref, lhs_ref, tm)

    lhs = lhs_ref[...]
    pb = jnp.dot(lhs_ref[0:tm, :], wb_ref[...], preferred_element_type=F32)
    pc = jnp.dot(lhs, wc_ref[...], preferred_element_type=F32)
    pv = jnp.dot(lhs, wv_ref[...], preferred_element_type=F32)
    o_ref[0] = (pb * _conv3(pc * pv, cw_ref[...], tm)).astype(BF16)


def _ffn_up_kernel(x_ref, xp_ref, xn_ref, sc_ref, sh_ref, wg_ref, wu_ref,
                   cwg_ref, cwu_ref, cbg_ref, cbu_ref, o_ref, lhs_ref, *, tm):
    @pl.when(pl.program_id(2) == 0)
    def _():
        _fill_lhs(x_ref, xp_ref, xn_ref, sc_ref, sh_ref, lhs_ref, tm)

    lhs = lhs_ref[...]
    g = _conv3(jnp.dot(lhs, wg_ref[...], preferred_element_type=F32), cwg_ref[...], tm) + cbg_ref[...]
    u = _conv3(jnp.dot(lhs, wu_ref[...], preferred_element_type=F32), cwu_ref[...], tm) + cbu_ref[...]
    o_ref[0] = (g * jax.nn.sigmoid(g) * u).astype(BF16)


def _halo_specs(n, d, tm):
    r = tm // HALO
    nblk = n // HALO
    return [
        pl.BlockSpec((1, tm, d), lambda b, i, j: (b, i, 0)),
        pl.BlockSpec((1, HALO, d), lambda b, i, j: (b, jnp.maximum(i * r - 1, 0), 0)),
        pl.BlockSpec((1, HALO, d), lambda b, i, j: (b, jnp.minimum((i + 1) * r, nblk - 1), 0)),
        pl.BlockSpec((1, 1, d), lambda b, i, j: (b, 0, 0)),
        pl.BlockSpec((1, 1, d), lambda b, i, j: (b, 0, 0)),
    ]


def _mixer_up_call(x, sc, sh, w_in, conv_w, tm=1024):
    bsz, n, d = x.shape
    tm = min(tm, n)
    fc = math.gcd(d, 512)
    nb = d // fc
    return pl.pallas_call(
        functools.partial(_mixer_up_kernel, tm=tm),
        out_shape=jax.ShapeDtypeStruct((bsz, n, d), BF16),
        grid=(bsz, n // tm, nb),
        in_specs=_halo_specs(n, d, tm) + [
            pl.BlockSpec((d, fc), lambda b, i, j: (0, j)),
            pl.BlockSpec((d, fc), lambda b, i, j: (0, nb + j)),
            pl.BlockSpec((d, fc), lambda b, i, j: (0, 2 * nb + j)),
            pl.BlockSpec((3, fc), lambda b, i, j: (0, j)),
        ],
        out_specs=pl.BlockSpec((1, tm, fc), lambda b, i, j: (b, i, j)),
        scratch_shapes=[pltpu.VMEM((tm + HALO, d), BF16)],
        compiler_params=_params("arbitrary", "arbitrary", "arbitrary"),
        name="mixer_up",
    )(x, x, x, sc, sh, w_in, w_in, w_in, conv_w)


def _ffn_up_call(x, sc, sh, w_up, conv_w, conv_b, tm=1024):
    bsz, n, d = x.shape
    f = w_up.shape[1] // 2
    tm = min(tm, n)
    fc = math.gcd(f, 512)
    nb = f // fc
    conv_b = conv_b.reshape(1, 2 * f)
    return pl.pallas_call(
        functools.partial(_ffn_up_kernel, tm=tm),
        out_shape=jax.ShapeDtypeStruct((bsz, n, f), BF16),
        grid=(bsz, n // tm, nb),
        in_specs=_halo_specs(n, d, tm) + [
            pl.BlockSpec((d, fc), lambda b, i, j: (0, j)),
            pl.BlockSpec((d, fc), lambda b, i, j: (0, nb + j)),
            pl.BlockSpec((3, fc), lambda b, i, j: (0, j)),
            pl.BlockSpec((3, fc), lambda b, i, j: (0, nb + j)),
            pl.BlockSpec((1, fc), lambda b, i, j: (0, j)),
            pl.BlockSpec((1, fc), lambda b, i, j: (0, nb + j)),
        ],
        out_specs=pl.BlockSpec((1, tm, fc), lambda b, i, j: (b, i, j)),
        scratch_shapes=[pltpu.VMEM((tm + HALO, d), BF16)],
        compiler_params=_params("arbitrary", "arbitrary", "arbitrary"),
        name="ffn_up",
    )(x, x, x, sc, sh, w_up, w_up, conv_w, conv_w, conv_b, conv_b)


def _proj_ln_kernel(a_ref, w_ref, x_ref, gate_ref, g_ref, b_ref, o_ref, *scratch, nk, alpha):
    def finalize(y):
        r = alpha * x_ref[0] + gate_ref[0] * y
        mu = jnp.mean(r, axis=-1, keepdims=True)
        rc = r - mu
        var = jnp.mean(rc * rc, axis=-1, keepdims=True)
        o_ref[0] = rc * lax.rsqrt(var + LN_EPS) * g_ref[...] + b_ref[...]

    part = jnp.dot(a_ref[0], w_ref[...], preferred_element_type=F32)
    if nk == 1:
        finalize(part)
    else:
        acc_ref, = scratch
        k = pl.program_id(2)

        @pl.when(k == 0)
        def _():
            acc_ref[...] = part

        @pl.when(jnp.logical_and(k > 0, k < nk - 1))
        def _():
            acc_ref[...] += part

        @pl.when(k == nk - 1)
        def _():
            finalize(acc_ref[...] + part)


def _k_chunk(k):
    if k <= 2048:
        return k
    best = 128
    for c in range(128, MAX_K_CHUNK + 1, 128):
        if k % c == 0:
            best = c
    return best


def _proj_ln_call(a, w, x, gate, g, b, alpha, tm=512):
    bsz, n, d = x.shape
    kdim = a.shape[-1]
    tm = min(tm, n)
    kc = _k_chunk(kdim)
    nk = kdim // kc
    return pl.pallas_call(
        functools.partial(_proj_ln_kernel, nk=nk, alpha=alpha),
        out_shape=jax.ShapeDtypeStruct((bsz, n, d), F32),
        grid=(bsz, n // tm, nk),
        in_specs=[
            pl.BlockSpec((1, tm, kc), lambda b_, i, k: (b_, i, k)),
            pl.BlockSpec((kc, d), lambda b_, i, k: (k, 0)),
            pl.BlockSpec((1, tm, d), lambda b_, i, k: (b_, i, 0)),
            pl.BlockSpec((1, 1, d), lambda b_, i, k: (b_, 0, 0)),
            pl.BlockSpec((1, d), lambda b_, i, k: (0, 0)),
            pl.BlockSpec((1, d), lambda b_, i, k: (0, 0)),
        ],
        out_specs=pl.BlockSpec((1, tm, d), lambda b_, i, k: (b_, i, 0)),
        scratch_shapes=[pltpu.VMEM((tm, d), F32)] if nk > 1 else [],
        compiler_params=_params("arbitrary", "arbitrary", "arbitrary"),
        name="proj_ln",
    )(a, w, x, gate, g.reshape(1, d), b.reshape(1, d))


def _rms(v, g):
    ms = jnp.mean(v * v, axis=-1, keepdims=True)
    return v * lax.rsqrt(ms + RMS_EPS) * g


def _rope128(t, cos, sin):
    return t * cos + pltpu.roll(t, QK_ROPE_DIM, 1) * sin


def _mla_down_kernel(x_ref, sc_ref, sh_ref, w_ref, qn_ref, kvn_ref, cos_ref, sin_ref,
                     ql_ref, ckv_ref, kr_ref, *, rq, rkv):
    h = (x_ref[0] * (1.0 + sc_ref[0]) + sh_ref[0]).astype(BF16)
    r = jnp.dot(h, w_ref[...], preferred_element_type=F32)
    ql_ref[0] = _rms(r[:, 0:rq], qn_ref[...]).astype(BF16)
    ckv_ref[0] = _rms(r[:, rq:rq + rkv], kvn_ref[...]).astype(BF16)
    t = r[:, rq + rkv:rq + rkv + 2 * QK_ROPE_DIM]
    kr_ref[0] = _rope128(t, cos_ref[...], sin_ref[...]).astype(BF16)


def _mla_down_call(x, sc, sh, w, q_norm, kv_norm, cos128, sin128, tm=1024):
    bsz, n, d = x.shape
    rq, rkv = q_norm.shape[-1], kv_norm.shape[-1]
    wn = w.shape[1]
    tm = min(tm, n)
    return pl.pallas_call(
        functools.partial(_mla_down_kernel, rq=rq, rkv=rkv),
        out_shape=(jax.ShapeDtypeStruct((bsz, n, rq), BF16),
                   jax.ShapeDtypeStruct((bsz, n, rkv), BF16),
                   jax.ShapeDtypeStruct((bsz, n, 2 * QK_ROPE_DIM), BF16)),
        grid=(bsz, n // tm),
        in_specs=[
            pl.BlockSpec((1, tm, d), lambda b, i: (b, i, 0)),
            pl.BlockSpec((1, 1, d), lambda b, i: (b, 0, 0)),
            pl.BlockSpec((1, 1, d), lambda b, i: (b, 0, 0)),
            pl.BlockSpec((d, wn), lambda b, i: (0, 0)),
            pl.BlockSpec((1, rq), lambda b, i: (0, 0)),
            pl.BlockSpec((1, rkv), lambda b, i: (0, 0)),
            pl.BlockSpec((tm, 2 * QK_ROPE_DIM), lambda b, i: (i, 0)),
            pl.BlockSpec((tm, 2 * QK_ROPE_DIM), lambda b, i: (i, 0)),
        ],
        out_specs=(pl.BlockSpec((1, tm, rq), lambda b, i: (b, i, 0)),
                   pl.BlockSpec((1, tm, rkv), lambda b, i: (b, i, 0)),
                   pl.BlockSpec((1, tm, 2 * QK_ROPE_DIM), lambda b, i: (b, i, 0))),
        compiler_params=_params("arbitrary", "arbitrary"),
        name="mla_down",
    )(x, sc, sh, w, q_norm.reshape(1, rq), kv_norm.reshape(1, rkv), cos128, sin128)


def _q_up_kernel(ql_ref, w_ref, cos_ref, sin_ref, o_ref, *, n_heads, qscale):
    ql = ql_ref[0]
    cos = cos_ref[...] * qscale
    sin = sin_ref[...] * qscale
    wh = QK_NOPE_DIM + 2 * QK_ROPE_DIM
    for h in range(n_heads):
        r = jnp.dot(ql, w_ref[:, h * wh:(h + 1) * wh], preferred_element_type=F32)
        o_ref[0, h, :, 0:QK_NOPE_DIM] = (r[:, 0:QK_NOPE_DIM] * qscale).astype(BF16)
        rope = _rope128(r[:, QK_NOPE_DIM:wh], cos, sin)
        o_ref[0, h, :, QK_NOPE_DIM:QK_HEAD_DIM] = rope[:, 0:QK_ROPE_DIM].astype(BF16)


def _q_up_call(ql, wq, cos128, sin128, n_heads, tm=1024):
    bsz, n, rq = ql.shape
    tm = min(tm, n)
    qscale = math.log2(math.e) / math.sqrt(QK_HEAD_DIM)
    return pl.pallas_call(
        functools.partial(_q_up_kernel, n_heads=n_heads, qscale=qscale),
        out_shape=jax.ShapeDtypeStruct((bsz, n_heads, n, QK_HEAD_DIM), BF16),
        grid=(bsz, n // tm),
        in_specs=[
            pl.BlockSpec((1, tm, rq), lambda b, i: (b, i, 0)),
            pl.BlockSpec(wq.shape, lambda b, i: (0, 0)),
            pl.BlockSpec((tm, 2 * QK_ROPE_DIM), lambda b, i: (i, 0)),
            pl.BlockSpec((tm, 2 * QK_ROPE_DIM), lambda b, i: (i, 0)),
        ],
        out_specs=pl.BlockSpec((1, n_heads, tm, QK_HEAD_DIM), lambda b, i: (b, 0, i, 0)),
        compiler_params=_params("arbitrary", "arbitrary"),
        name="q_up",
    )(ql, wq, cos128, sin128)


def _kv_up_kernel(ckv_ref, kr_ref, w_ref, k_ref, v_ref, *, n_heads):
    ckv = ckv_ref[0]
    kr = kr_ref[0][:, 0:QK_ROPE_DIM]
    wh = QK_NOPE_DIM + V_HEAD_DIM
    for h in range(n_heads):
        r = jnp.dot(ckv, w_ref[:, h * wh:(h + 1) * wh], preferred_element_type=F32)
        k_ref[0, h, :, 0:QK_NOPE_DIM] = r[:, 0:QK_NOPE_DIM].astype(BF16)
        k_ref[0, h, :, QK_NOPE_DIM:QK_HEAD_DIM] = kr
        v_ref[0, h] = r[:, QK_NOPE_DIM:wh].astype(BF16)


def _kv_up_call(ckv, kr, w_kv, n_heads, tm=1024):
    bsz, n, rkv = ckv.shape
    tm = min(tm, n)
    return pl.pallas_call(
        functools.partial(_kv_up_kernel, n_heads=n_heads),
        out_shape=(jax.ShapeDtypeStruct((bsz, n_heads, n, QK_HEAD_DIM), BF16),
                   jax.ShapeDtypeStruct((bsz, n_heads, n, V_HEAD_DIM), BF16)),
        grid=(bsz, n // tm),
        in_specs=[
            pl.BlockSpec((1, tm, rkv), lambda b, i: (b, i, 0)),
            pl.BlockSpec((1, tm, 2 * QK_ROPE_DIM), lambda b, i: (b, i, 0)),
            pl.BlockSpec(w_kv.shape, lambda b, i: (0, 0)),
        ],
        out_specs=(pl.BlockSpec((1, n_heads, tm, QK_HEAD_DIM), lambda b, i: (b, 0, i, 0)),
                   pl.BlockSpec((1, n_heads, tm, V_HEAD_DIM), lambda b, i: (b, 0, i, 0))),
        compiler_params=_params("arbitrary", "arbitrary"),
        name="kv_up",
    )(ckv, kr, w_kv)


_NT_DIMS = (((1,), (1,)), ((), ()))


def _tile_lanes(v, width):
    return jnp.concatenate([v] * (width // LANES), axis=1) if width > LANES else v


def _lane_tile_max(s):
    pm = s[:, 0:LANES]
    for c0 in range(LANES, s.shape[1], LANES):
        pm = jnp.maximum(pm, s[:, c0:c0 + LANES])
    return pm


def _ctx_attn_kernel(q_ref, kc_ref, vc_ref, o_ref):
    s = lax.dot_general(q_ref[0, 0], kc_ref[0, 0], _NT_DIMS, preferred_element_type=F32)
    p = jnp.exp2(s - jnp.max(s, axis=1, keepdims=True))
    o = jnp.dot(p.astype(BF16), vc_ref[0, 0], preferred_element_type=F32)
    o_ref[0] = (o / jnp.sum(p, axis=1, keepdims=True)).astype(BF16)


def _attn_kernel(q_ref, kc_ref, vc_ref, k_ref, v_ref, o_ref,
                 m_ref, l_ref, acc_ref, s_ref, pm_ref, p_ref, al_ref, *, tk, n_chunks, rb):
    tq = q_ref.shape[2]

    def scores(j, slot):
        start = pl.multiple_of(j * tk, tk)
        s = lax.dot_general(q_ref[0, 0], k_ref[0, 0, pl.ds(start, tk), :], _NT_DIMS,
                            preferred_element_type=F32)
        s_ref[slot] = s
        pm_ref[slot] = _lane_tile_max(s)

    def softmax(slot):
        m_old = m_ref[...]
        m_new = jnp.maximum(m_old, jnp.max(pm_ref[slot], axis=1, keepdims=True))
        m_ref[...] = m_new
        al_ref[slot] = jnp.exp2(m_old - m_new)
        for r0 in range(0, tq, rb):
            rows = slice(r0, r0 + rb)
            p = jnp.exp2(s_ref[slot, rows, :] - _tile_lanes(m_ref[rows, :], tk))
            p_ref[slot, rows, :] = p.astype(BF16)
            l_ref[rows, :] = al_ref[slot, rows, :] * l_ref[rows, :] + jnp.sum(p, axis=1, keepdims=True)

    def weighted_values(j, slot):
        start = pl.multiple_of(j * tk, tk)
        pv = jnp.dot(p_ref[slot], v_ref[0, 0, pl.ds(start, tk), :], preferred_element_type=F32)
        acc_ref[...] = al_ref[slot] * acc_ref[...] + pv

    s = lax.dot_general(q_ref[0, 0], kc_ref[0, 0], _NT_DIMS, preferred_element_type=F32)
    m0 = jnp.broadcast_to(jnp.max(s, axis=1, keepdims=True), (tq, LANES))
    p = jnp.exp2(s - _tile_lanes(m0, s.shape[1]))
    m_ref[...] = m0
    l_ref[...] = jnp.broadcast_to(jnp.sum(p, axis=1, keepdims=True), (tq, LANES))
    acc_ref[...] = jnp.dot(p.astype(BF16), vc_ref[0, 0], preferred_element_type=F32)

    scores(0, 0)
    softmax(0)
    scores(1, 1)

    def body(i, carry):
        j = 2 * i + 1
        weighted_values(j - 1, 0)
        scores(j + 1, 0)
        softmax(1)
        weighted_values(j, 1)
        scores(j + 2, 1)
        softmax(0)
        return carry

    lax.fori_loop(0, (n_chunks - 2) // 2, body, 0)
    weighted_values(n_chunks - 2, 0)
    softmax(1)
    weighted_values(n_chunks - 1, 1)
    o_ref[0] = (acc_ref[...] / l_ref[...]).astype(BF16)


def _attn_call(q, k_c, v_c, k_x=None, v_x=None, tq=2048, tk=512, rb=ROW_BLOCK):
    bsz, n_heads, n, _ = q.shape
    lc = k_c.shape[2]
    tq = min(tq, n)
    in_specs = [
        pl.BlockSpec((1, 1, tq, QK_HEAD_DIM), lambda b, h, i: (b, h, i, 0)),
        pl.BlockSpec((1, 1, lc, QK_HEAD_DIM), lambda b, h, i: (b, h, 0, 0)),
        pl.BlockSpec((1, 1, lc, V_HEAD_DIM), lambda b, h, i: (b, h, 0, 0)),
    ]
    out_shape = jax.ShapeDtypeStruct((bsz, n, n_heads * V_HEAD_DIM), BF16)
    out_specs = pl.BlockSpec((1, tq, V_HEAD_DIM), lambda b, h, i: (b, i, h))
    if k_x is None:
        return pl.pallas_call(
            _ctx_attn_kernel, out_shape=out_shape, grid=(bsz, n_heads, n // tq),
            in_specs=in_specs, out_specs=out_specs,
            compiler_params=_params("arbitrary", "arbitrary", "arbitrary"),
            name="ctx_attn",
        )(q, k_c, v_c)
    nk = k_x.shape[2]
    tk = min(tk, nk // 2)
    n_chunks = nk // tk
    assert n_chunks % 2 == 0 and n_chunks * tk == nk
    in_specs += [
        pl.BlockSpec((1, 1, nk, QK_HEAD_DIM), lambda b, h, i: (b, h, 0, 0)),
        pl.BlockSpec((1, 1, nk, V_HEAD_DIM), lambda b, h, i: (b, h, 0, 0)),
    ]
    return pl.pallas_call(
        functools.partial(_attn_kernel, tk=tk, n_chunks=n_chunks, rb=rb),
        out_shape=out_shape,
        grid=(bsz, n_heads, n // tq),
        in_specs=in_specs,
        out_specs=out_specs,
        scratch_shapes=[pltpu.VMEM((tq, LANES), F32),
                        pltpu.VMEM((tq, LANES), F32),
                        pltpu.VMEM((tq, V_HEAD_DIM), F32),
                        pltpu.VMEM((2, tq, tk), F32),
                        pltpu.VMEM((2, tq, LANES), F32),
                        pltpu.VMEM((2, tq, tk), BF16),
                        pltpu.VMEM((2, tq, LANES), F32)],
        compiler_params=_params("arbitrary", "arbitrary", "arbitrary"),
        name="mla_attn",
    )(q, k_c, v_c, k_x, v_x)


def _rot_cols(w):
    q = QK_ROPE_DIM // 4
    return jnp.concatenate([-w[..., q:2 * q], w[..., 0:q], -w[..., 3 * q:4 * q], w[..., 2 * q:3 * q]], -1)


def _rope_tables(n):
    rows = n // GRID_W
    n_freq = QK_ROPE_DIM // 4
    inv_freq = 1.0 / (ROPE_THETA ** (jnp.arange(n_freq, dtype=F32) / n_freq))
    row = jnp.broadcast_to(jnp.arange(rows, dtype=F32)[:, None], (rows, GRID_W)).reshape(-1)
    col = jnp.broadcast_to(jnp.arange(GRID_W, dtype=F32)[None, :], (rows, GRID_W)).reshape(-1)
    ang_r = row[:, None] * inv_freq[None, :]
    ang_c = col[:, None] * inv_freq[None, :]
    ang = jnp.concatenate([ang_r, ang_r, ang_c, ang_c], -1)
    return jnp.cos(ang), jnp.sin(ang)


def kernel(x, c, ctx, c_ctx, ada_w, ada_b, ln1_g, ln1_b, ln2_g, ln2_b, sc_w_in, sc_conv_w, sc_w_out, mla_w_dq, mla_q_norm, mla_w_uq, mla_w_dkv, mla_kv_norm, mla_w_uk, mla_w_uv, mla_w_o, ffn_w_up, ffn_conv_w, ffn_conv_b, ffn_w_down):
    bsz, n, d = x.shape
    lc = ctx.shape[1]
    depth = ada_w.shape[0]
    n_heads = N_HEADS
    alpha = (2.0 * depth) ** 0.25

    crow = jnp.concatenate([c, c_ctx[None, :], jnp.zeros((8 - bsz - 1, d), F32)], 0)
    mods = _ada_call(crow, ada_w, ada_b)

    def mod_x(i, k):
        return mods[i, 0:bsz, k * d:(k + 1) * d].reshape(bsz, 1, d)

    def mod_c(i, k):
        return jnp.broadcast_to(mods[i, bsz:bsz + 1, k * d:(k + 1) * d].reshape(1, 1, d), (bsz, 1, d))

    cos, sin = _rope_tables(n)
    zpad = jnp.zeros((n, QK_ROPE_DIM), F32)
    cos128_x = jnp.concatenate([cos, zpad], 1)
    sin128_x = jnp.concatenate([sin, zpad], 1)
    cos128_c = jnp.concatenate([jnp.ones((lc, QK_ROPE_DIM), F32), jnp.zeros((lc, QK_ROPE_DIM), F32)], 1)
    sin128_c = jnp.zeros((lc, 2 * QK_ROPE_DIM), F32)

    for i in range(depth):
        last = i == depth - 1
        j = i // 2
        sh1, sc1, g1, sh2, sc2, g2 = (mod_x(i, k) for k in range(6))
        csh1, csc1, cg1, csh2, csc2, cg2 = (mod_c(i, k) for k in range(6))
        if i % 2 == 0:
            w_in = sc_w_in[j].astype(BF16)
            w_out = sc_w_out[j].astype(BF16)
            zx = _mixer_up_call(x, sc1, sh1, w_in, sc_conv_w[j])
            x1 = _proj_ln_call(zx, w_out, x, g1, ln1_g[i], ln1_b[i], alpha)
            if not last:
                zc = _mixer_up_call(ctx, csc1, csh1, w_in, sc_conv_w[j])
                ctx1 = _proj_ln_call(zc, w_out, ctx, cg1, ln1_g[i], ln1_b[i], alpha)
        else:
            rq = mla_w_dq.shape[-1]
            rkv = mla_kv_norm.shape[-1]
            w_dkv = mla_w_dkv[j]
            w_down = jnp.concatenate([mla_w_dq[j], w_dkv, _rot_cols(w_dkv[:, rkv:])], 1).astype(BF16)
            wq = mla_w_uq[j].reshape(rq, n_heads, QK_HEAD_DIM)
            wq = jnp.concatenate([wq, _rot_cols(wq[..., QK_NOPE_DIM:])], -1)
            wq = wq.reshape(rq, n_heads * (QK_HEAD_DIM + QK_ROPE_DIM)).astype(BF16)
            w_kv = jnp.concatenate([mla_w_uk[j].reshape(rkv, n_heads, QK_NOPE_DIM),
                                    mla_w_uv[j].reshape(rkv, n_heads, V_HEAD_DIM)], -1)
            w_kv = w_kv.reshape(rkv, n_heads * (QK_NOPE_DIM + V_HEAD_DIM)).astype(BF16)
            w_o = mla_w_o[j].astype(BF16)

            ql_x, ckv_x, kr_x = _mla_down_call(x, sc1, sh1, w_down, mla_q_norm[j], mla_kv_norm[j],
                                               cos128_x, sin128_x)
            ql_c, ckv_c, kr_c = _mla_down_call(ctx, csc1, csh1, w_down, mla_q_norm[j], mla_kv_norm[j],
                                               cos128_c, sin128_c)
            k_x, v_x = _kv_up_call(ckv_x, kr_x, w_kv, n_heads)
            k_c, v_c = _kv_up_call(ckv_c, kr_c, w_kv, n_heads)
            q_x = _q_up_call(ql_x, wq, cos128_x, sin128_x, n_heads)
            o_x = _attn_call(q_x, k_c, v_c, k_x, v_x)
            x1 = _proj_ln_call(o_x, w_o, x, g1, ln1_g[i], ln1_b[i], alpha)
            if not last:
                q_c = _q_up_call(ql_c, wq, cos128_c, sin128_c, n_heads)
                o_c = _attn_call(q_c, k_c, v_c)
                ctx1 = _proj_ln_call(o_c, w_o, ctx, cg1, ln1_g[i], ln1_b[i], alpha)

        w_up = ffn_w_up[i].astype(BF16)
        w_dn = ffn_w_down[i].astype(BF16)
        ax = _ffn_up_call(x1, sc2, sh2, w_up, ffn_conv_w[i], ffn_conv_b[i])
        x = _proj_ln_call(ax, w_dn, x1, g2, ln2_g[i], ln2_b[i], alpha)
        if not last:
            ac = _ffn_up_call(ctx1, csc2, csh2, w_up, ffn_conv_w[i], ffn_conv_b[i])
            ctx = _proj_ln_call(ac, w_dn, ctx1, cg2, ln2_g[i], ln2_b[i], alpha)
    return x
```

```python
import functools
import math

import jax
import jax.numpy as jnp
from jax import lax
from jax.experimental import pallas as pl
from jax.experimental.pallas import tpu as pltpu

F32 = jnp.float32
BF16 = jnp.bfloat16

GRID_W = 64
N_HEADS = 16
QK_NOPE_DIM = 128
QK_ROPE_DIM = 64
QK_HEAD_DIM = QK_NOPE_DIM + QK_ROPE_DIM
V_HEAD_DIM = 128
ROPE_THETA = 10000.0
LN_EPS = 1e-5
RMS_EPS = 1e-6

LANES = 128
HALO = 16
ROW_BLOCK = 32
VMEM_LIMIT = 56 * 1024 * 1024


def _params(*sem):
    return pltpu.CompilerParams(dimension_semantics=sem, vmem_limit_bytes=VMEM_LIMIT)


def _ada_kernel(c_ref, w_ref, b_ref, o_ref):
    c = c_ref[...]
    s = c * jax.nn.sigmoid(c)
    o_ref[0] = jnp.dot(s, w_ref[0], precision=lax.Precision.HIGHEST,
                       preferred_element_type=F32) + b_ref[0]


def _ada_call(crow, ada_w, ada_b):
    depth, d, n6 = ada_w.shape
    tn = math.gcd(n6, 1024)
    return pl.pallas_call(
        _ada_kernel,
        out_shape=jax.ShapeDtypeStruct((depth, 8, n6), F32),
        grid=(depth, n6 // tn),
        in_specs=[
            pl.BlockSpec((8, d), lambda l, j: (0, 0)),
            pl.BlockSpec((1, d, tn), lambda l, j: (l, 0, j)),
            pl.BlockSpec((1, 1, tn), lambda l, j: (l, 0, j)),
        ],
        out_specs=pl.BlockSpec((1, 8, tn), lambda l, j: (l, 0, j)),
        compiler_params=_params("arbitrary", "arbitrary"),
        name="ada_mod",
    )(crow, ada_w, ada_b.reshape(depth, 1, n6))


def _fill_lhs(x_ref, xp_ref, xn_ref, sc_ref, sh_ref, lhs_ref, tm):
    i = pl.program_id(1)
    last = pl.num_programs(1) - 1
    scale = 1.0 + sc_ref[0]
    shift = sh_ref[0]
    lhs_ref[0:tm, :] = (x_ref[0] * scale + shift).astype(BF16)
    row = lax.broadcasted_iota(jnp.int32, (HALO, 1), 0)
    hn = xn_ref[0] * scale + shift
    hp = xp_ref[0] * scale + shift
    row_n = jnp.where(i < last, 0, -1)
    row_p = jnp.where(i > 0, HALO - 1, -1)
    halo = jnp.where(row == row_n, hn, jnp.where(row == row_p, hp, 0.0))
    lhs_ref[tm:tm + HALO, :] = halo.astype(BF16)


def _conv3(p, cw, tm):
    rows = p.shape[0]
    prev = pltpu.roll(p, 1, 0)[0:tm]
    nxt = pltpu.roll(p, rows - 1, 0)[0:tm]
    return cw[0:1] * prev + cw[1:2] * p[0:tm] + cw[2:3] * nxt


def _mixer_up_kernel(x_ref, xp_ref, xn_ref, sc_ref, sh_ref, wb_ref, wc_ref, wv_ref,
                     cw_ref, o_ref, lhs_ref, *, tm):
    @pl.when(pl.program_id(2) == 0)
    def _():
        _fill_lhs(x_ref, xp_ref, xn_ref, sc_ref, sh_ref, lhs_ref, tm)

    lhs = lhs_ref[...]
    pb = jnp.dot(lhs_ref[0:tm, :], wb_ref[...], preferred_element_type=F32)
    pc = jnp.dot(lhs, wc_ref[...], preferred_element_type=F32)
    pv = jnp.dot(lhs, wv_ref[...], preferred_element_type=F32)
    o_ref[0] = (pb * _conv3(pc * pv, cw_ref[...], tm)).astype(BF16)


def _ffn_up_kernel(x_ref, xp_ref, xn_ref, sc_ref, sh_ref, wg_ref, wu_ref,
                   cwg_ref, cwu_ref, cbg_ref, cbu_ref, o_ref, lhs_ref, *, tm):
    @pl.when(pl.program_id(2) == 0)
    def _():
        _fill_lhs(x_ref, xp_ref, xn_ref, sc_ref, sh_ref, lhs_ref, tm)

    lhs = lhs_ref[...]
    g = _conv3(jnp.dot(lhs, wg_ref[...], preferred_element_type=F32), cwg_ref[...], tm) + cbg_ref[...]
    u = _conv3(jnp.dot(lhs, wu_ref[...], preferred_element_type=F32), cwu_ref[...], tm) + cbu_ref[...]
    o_ref[0] = (g * jax.nn.sigmoid(g) * u).astype(BF16)


def _halo_specs(n, d, tm):
    r = tm // HALO
    nblk = n // HALO
    return [
        pl.BlockSpec((1, tm, d), lambda b, i, j: (b, i, 0)),
        pl.BlockSpec((1, HALO, d), lambda b, i, j: (b, jnp.maximum(i * r - 1, 0), 0)),
        pl.BlockSpec((1, HALO, d), lambda b, i, j: (b, jnp.minimum((i + 1) * r, nblk - 1), 0)),
        pl.BlockSpec((1, 1, d), lambda b, i, j: (b, 0, 0)),
        pl.BlockSpec((1, 1, d), lambda b, i, j: (b, 0, 0)),
    ]


def _mixer_up_call(x, sc, sh, w_in, conv_w, tm=1024):
    bsz, n, d = x.shape
    tm = min(tm, n)
    fc = math.gcd(d, 512)
    nb = d // fc
    return pl.pallas_call(
        functools.partial(_mixer_up_kernel, tm=tm),
        out_shape=jax.ShapeDtypeStruct((bsz, n, d), BF16),
        grid=(bsz, n // tm, nb),
        in_specs=_halo_specs(n, d, tm) + [
            pl.BlockSpec((d, fc), lambda b, i, j: (0, j)),
            pl.BlockSpec((d, fc), lambda b, i, j: (0, nb + j)),
            pl.BlockSpec((d, fc), lambda b, i, j: (0, 2 * nb + j)),
            pl.BlockSpec((3, fc), lambda b, i, j: (0, j)),
        ],
        out_specs=pl.BlockSpec((1, tm, fc), lambda b, i, j: (b, i, j)),
        scratch_shapes=[pltpu.VMEM((tm + HALO, d), BF16)],
        compiler_params=_params("arbitrary", "arbitrary", "arbitrary"),
        name="mixer_up",
    )(x, x, x, sc, sh, w_in, w_in, w_in, conv_w)


def _ffn_up_call(x, sc, sh, w_up, conv_w, conv_b, tm=1024):
    bsz, n, d = x.shape
    f = w_up.shape[1] // 2
    tm = min(tm, n)
    fc = math.gcd(f, 512)
    nb = f // fc
    conv_b = conv_b.reshape(1, 2 * f)
    return pl.pallas_call(
        functools.partial(_ffn_up_kernel, tm=tm),
        out_shape=jax.ShapeDtypeStruct((bsz, n, f), BF16),
        grid=(bsz, n // tm, nb),
        in_specs=_halo_specs(n, d, tm) + [
            pl.BlockSpec((d, fc), lambda b, i, j: (0, j)),
            pl.BlockSpec((d, fc), lambda b, i, j: (0, nb + j)),
            pl.BlockSpec((3, fc), lambda b, i, j: (0, j)),
            pl.BlockSpec((3, fc), lambda b, i, j: (0, nb + j)),
            pl.BlockSpec((1, fc), lambda b, i, j: (0, j)),
            pl.BlockSpec((1, fc), lambda b, i, j: (0, nb + j)),
        ],
        out_specs=pl.BlockSpec((1, tm, fc), lambda b, i, j: (b, i, j)),
        scratch_shapes=[pltpu.VMEM((tm + HALO, d), BF16)],
        compiler_params=_params("arbitrary", "arbitrary", "arbitrary"),
        name="ffn_up",
    )(x, x, x, sc, sh, w_up, w_up, conv_w, conv_w, conv_b, conv_b)


def _proj_ln_kernel(a_ref, w_ref, x_ref, gate_ref, g_ref, b_ref, o_ref, *, alpha):
    y = jnp.dot(a_ref[0], w_ref[...], preferred_element_type=F32)
    r = alpha * x_ref[0] + gate_ref[0] * y
    mu = jnp.mean(r, axis=-1, keepdims=True)
    rc = r - mu
    var = jnp.mean(rc * rc, axis=-1, keepdims=True)
    o_ref[0] = rc * lax.rsqrt(var + LN_EPS) * g_ref[...] + b_ref[...]


def _proj_ln_call(a, w, x, gate, g, b, alpha, tm=512):
    bsz, n, d = x.shape
    kdim = a.shape[-1]
    tm = min(tm, n)
    return pl.pallas_call(
        functools.partial(_proj_ln_kernel, alpha=alpha),
        out_shape=jax.ShapeDtypeStruct((bsz, n, d), F32),
        grid=(bsz, n // tm),
        in_specs=[
            pl.BlockSpec((1, tm, kdim), lambda b_, i: (b_, i, 0)),
            pl.BlockSpec((kdim, d), lambda b_, i: (0, 0), pipeline_mode=pl.Buffered(1)),
            pl.BlockSpec((1, tm, d), lambda b_, i: (b_, i, 0)),
            pl.BlockSpec((1, 1, d), lambda b_, i: (b_, 0, 0)),
            pl.BlockSpec((1, d), lambda b_, i: (0, 0)),
            pl.BlockSpec((1, d), lambda b_, i: (0, 0)),
        ],
        out_specs=pl.BlockSpec((1, tm, d), lambda b_, i: (b_, i, 0)),
        compiler_params=_params("arbitrary", "arbitrary"),
        name="proj_ln",
    )(a, w, x, gate, g.reshape(1, d), b.reshape(1, d))


def _rms(v, g):
    ms = jnp.mean(v * v, axis=-1, keepdims=True)
    return v * lax.rsqrt(ms + RMS_EPS) * g


def _rope128(t, cos, sin):
    return t * cos + pltpu.roll(t, QK_ROPE_DIM, 1) * sin


def _mla_down_kernel(x_ref, sc_ref, sh_ref, w_ref, qn_ref, kvn_ref, cos_ref, sin_ref,
                     ql_ref, ckv_ref, kr_ref, *, rq, rkv):
    h = (x_ref[0] * (1.0 + sc_ref[0]) + sh_ref[0]).astype(BF16)
    r = jnp.dot(h, w_ref[...], preferred_element_type=F32)
    ql_ref[0] = _rms(r[:, 0:rq], qn_ref[...]).astype(BF16)
    ckv_ref[0] = _rms(r[:, rq:rq + rkv], kvn_ref[...]).astype(BF16)
    t = r[:, rq + rkv:rq + rkv + 2 * QK_ROPE_DIM]
    kr_ref[0] = _rope128(t, cos_ref[...], sin_ref[...]).astype(BF16)


def _mla_down_call(x, sc, sh, w, q_norm, kv_norm, cos128, sin128, tm=1024):
    bsz, n, d = x.shape
    rq, rkv = q_norm.shape[-1], kv_norm.shape[-1]
    wn = w.shape[1]
    tm = min(tm, n)
    return pl.pallas_call(
        functools.partial(_mla_down_kernel, rq=rq, rkv=rkv),
        out_shape=(jax.ShapeDtypeStruct((bsz, n, rq), BF16),
                   jax.ShapeDtypeStruct((bsz, n, rkv), BF16),
                   jax.ShapeDtypeStruct((bsz, n, 2 * QK_ROPE_DIM), BF16)),
        grid=(bsz, n // tm),
        in_specs=[
            pl.BlockSpec((1, tm, d), lambda b, i: (b, i, 0)),
            pl.BlockSpec((1, 1, d), lambda b, i: (b, 0, 0)),
            pl.BlockSpec((1, 1, d), lambda b, i: (b, 0, 0)),
            pl.BlockSpec((d, wn), lambda b, i: (0, 0)),
            pl.BlockSpec((1, rq), lambda b, i: (0, 0)),
            pl.BlockSpec((1, rkv), lambda b, i: (0, 0)),
            pl.BlockSpec((tm, 2 * QK_ROPE_DIM), lambda b, i: (i, 0)),
            pl.BlockSpec((tm, 2 * QK_ROPE_DIM), lambda b, i: (i, 0)),
        ],
        out_specs=(pl.BlockSpec((1, tm, rq), lambda b, i: (b, i, 0)),
                   pl.BlockSpec((1, tm, rkv), lambda b, i: (b, i, 0)),
                   pl.BlockSpec((1, tm, 2 * QK_ROPE_DIM), lambda b, i: (b, i, 0))),
        compiler_params=_params("arbitrary", "arbitrary"),
        name="mla_down",
    )(x, sc, sh, w, q_norm.reshape(1, rq), kv_norm.reshape(1, rkv), cos128, sin128)


def _q_up_kernel(ql_ref, w_ref, cos_ref, sin_ref, o_ref, *, n_heads, qscale):
    ql = ql_ref[0]
    cos = cos_ref[...] * qscale
    sin = sin_ref[...] * qscale
    wh = QK_NOPE_DIM + 2 * QK_ROPE_DIM
    for h in range(n_heads):
        r = jnp.dot(ql, w_ref[:, h * wh:(h + 1) * wh], preferred_element_type=F32)
        o_ref[0, h, :, 0:QK_NOPE_DIM] = (r[:, 0:QK_NOPE_DIM] * qscale).astype(BF16)
        rope = _rope128(r[:, QK_NOPE_DIM:wh], cos, sin)
        o_ref[0, h, :, QK_NOPE_DIM:QK_HEAD_DIM] = rope[:, 0:QK_ROPE_DIM].astype(BF16)


def _q_up_call(ql, wq, cos128, sin128, n_heads, tm=1024):
    bsz, n, rq = ql.shape
    tm = min(tm, n)
    qscale = math.log2(math.e) / math.sqrt(QK_HEAD_DIM)
    return pl.pallas_call(
        functools.partial(_q_up_kernel, n_heads=n_heads, qscale=qscale),
        out_shape=jax.ShapeDtypeStruct((bsz, n_heads, n, QK_HEAD_DIM), BF16),
        grid=(bsz, n // tm),
        in_specs=[
            pl.BlockSpec((1, tm, rq), lambda b, i: (b, i, 0)),
            pl.BlockSpec(wq.shape, lambda b, i: (0, 0)),
            pl.BlockSpec((tm, 2 * QK_ROPE_DIM), lambda b, i: (i, 0)),
            pl.BlockSpec((tm, 2 * QK_ROPE_DIM), lambda b, i: (i, 0)),
        ],
        out_specs=pl.BlockSpec((1, n_heads, tm, QK_HEAD_DIM), lambda b, i: (b, 0, i, 0)),
        compiler_params=_params("arbitrary", "arbitrary"),
        name="q_up",
    )(ql, wq, cos128, sin128)


def _kv_up_kernel(ckv_ref, kr_ref, w_ref, k_ref, v_ref, *, n_heads):
    ckv = ckv_ref[0]
    kr = kr_ref[0][:, 0:QK_ROPE_DIM]
    wh = QK_NOPE_DIM + V_HEAD_DIM
    for h in range(n_heads):
        r = jnp.dot(ckv, w_ref[:, h * wh:(h + 1) * wh], preferred_element_type=F32)
        k_ref[0, h, :, 0:QK_NOPE_DIM] = r[:, 0:QK_NOPE_DIM].astype(BF16)
        k_ref[0, h, :, QK_NOPE_DIM:QK_HEAD_DIM] = kr
        v_ref[0, h] = r[:, QK_NOPE_DIM:wh].astype(BF16)


def _kv_up_call(ckv, kr, w_kv, n_heads, tm=1024):
    bsz, n, rkv = ckv.shape
    tm = min(tm, n)
    return pl.pallas_call(
        functools.partial(_kv_up_kernel, n_heads=n_heads),
        out_shape=(jax.ShapeDtypeStruct((bsz, n_heads, n, QK_HEAD_DIM), BF16),
                   jax.ShapeDtypeStruct((bsz, n_heads, n, V_HEAD_DIM), BF16)),
        grid=(bsz, n // tm),
        in_specs=[
            pl.BlockSpec((1, tm, rkv), lambda b, i: (b, i, 0)),
            pl.BlockSpec((1, tm, 2 * QK_ROPE_DIM), lambda b, i: (b, i, 0)),
            pl.BlockSpec(w_kv.shape, lambda b, i: (0, 0)),
        ],
        out_specs=(pl.BlockSpec((1, n_heads, tm, QK_HEAD_DIM), lambda b, i: (b, 0, i, 0)),
                   pl.BlockSpec((1, n_heads, tm, V_HEAD_DIM), lambda b, i: (b, 0, i, 0))),
        compiler_params=_params("arbitrary", "arbitrary"),
        name="kv_up",
    )(ckv, kr, w_kv)


_NT_DIMS = (((1,), (1,)), ((), ()))


def _tile_lanes(v, width):
    return jnp.concatenate([v] * (width // LANES), axis=1) if width > LANES else v


def _lane_tile_max(s):
    pm = s[:, 0:LANES]
    for c0 in range(LANES, s.shape[1], LANES):
        pm = jnp.maximum(pm, s[:, c0:c0 + LANES])
    return pm


def _ctx_attn_kernel(q_ref, kc_ref, vc_ref, o_ref):
    s = lax.dot_general(q_ref[0, 0], kc_ref[0, 0], _NT_DIMS, preferred_element_type=F32)
    p = jnp.exp2(s - jnp.max(s, axis=1, keepdims=True))
    o = jnp.dot(p.astype(BF16), vc_ref[0, 0], preferred_element_type=F32)
    o_ref[0] = (o / jnp.sum(p, axis=1, keepdims=True)).astype(BF16)


def _attn_kernel(q_ref, kc_ref, vc_ref, k_ref, v_ref, o_ref,
                 m_ref, l_ref, acc_ref, s_ref, pm_ref, p_ref, al_ref, *, tk, n_chunks, rb):
    tq = q_ref.shape[2]

    def scores(j, slot):
        start = pl.multiple_of(j * tk, tk)
        s = lax.dot_general(q_ref[0, 0], k_ref[0, 0, pl.ds(start, tk), :], _NT_DIMS,
                            preferred_element_type=F32)
        s_ref[slot] = s
        pm_ref[slot] = _lane_tile_max(s)

    def softmax(slot):
        m_old = m_ref[...]
        m_new = jnp.maximum(m_old, jnp.max(pm_ref[slot], axis=1, keepdims=True))
        m_ref[...] = m_new
        al_ref[slot] = jnp.exp2(m_old - m_new)
        for r0 in range(0, tq, rb):
            rows = slice(r0, r0 + rb)
            p = jnp.exp2(s_ref[slot, rows, :] - _tile_lanes(m_ref[rows, :], tk))
            p_ref[slot, rows, :] = p.astype(BF16)
            l_ref[rows, :] = al_ref[slot, rows, :] * l_ref[rows, :] + jnp.sum(p, axis=1, keepdims=True)

    def weighted_values(j, slot):
        start = pl.multiple_of(j * tk, tk)
        pv = jnp.dot(p_ref[slot], v_ref[0, 0, pl.ds(start, tk), :], preferred_element_type=F32)
        acc_ref[...] = al_ref[slot] * acc_ref[...] + pv

    s = lax.dot_general(q_ref[0, 0], kc_ref[0, 0], _NT_DIMS, preferred_element_type=F32)
    m0 = jnp.broadcast_to(jnp.max(s, axis=1, keepdims=True), (tq, LANES))
    p = jnp.exp2(s - _tile_lanes(m0, s.shape[1]))
    m_ref[...] = m0
    l_ref[...] = jnp.broadcast_to(jnp.sum(p, axis=1, keepdims=True), (tq, LANES))
    acc_ref[...] = jnp.dot(p.astype(BF16), vc_ref[0, 0], preferred_element_type=F32)

    scores(0, 0)
    softmax(0)
    scores(1, 1)

    def body(i, carry):
        j = 2 * i + 1
        weighted_values(j - 1, 0)
        scores(j + 1, 0)
        softmax(1)
        weighted_values(j, 1)
        scores(j + 2, 1)
        softmax(0)
        return carry

    lax.fori_loop(0, (n_chunks - 2) // 2, body, 0)
    weighted_values(n_chunks - 2, 0)
    softmax(1)
    weighted_values(n_chunks - 1, 1)
    o_ref[0] = (acc_ref[...] / l_ref[...]).astype(BF16)


def _attn_call(q, k_c, v_c, k_x=None, v_x=None, tq=2048, tk=512, rb=ROW_BLOCK):
    bsz, n_heads, n, _ = q.shape
    lc = k_c.shape[2]
    tq = min(tq, n)
    in_specs = [
        pl.BlockSpec((1, 1, tq, QK_HEAD_DIM), lambda b, h, i: (b, h, i, 0)),
        pl.BlockSpec((1, 1, lc, QK_HEAD_DIM), lambda b, h, i: (b, h, 0, 0)),
        pl.BlockSpec((1, 1, lc, V_HEAD_DIM), lambda b, h, i: (b, h, 0, 0)),
    ]
    out_shape = jax.ShapeDtypeStruct((bsz, n, n_heads * V_HEAD_DIM), BF16)
    out_specs = pl.BlockSpec((1, tq, V_HEAD_DIM), lambda b, h, i: (b, i, h))
    if k_x is None:
        return pl.pallas_call(
            _ctx_attn_kernel, out_shape=out_shape, grid=(bsz, n_heads, n // tq),
            in_specs=in_specs, out_specs=out_specs,
            compiler_params=_params("arbitrary", "arbitrary", "arbitrary"),
            name="ctx_attn",
        )(q, k_c, v_c)
    nk = k_x.shape[2]
    tk = min(tk, nk // 2)
    n_chunks = nk // tk
    assert n_chunks % 2 == 0 and n_chunks * tk == nk
    in_specs += [
        pl.BlockSpec((1, 1, nk, QK_HEAD_DIM), lambda b, h, i: (b, h, 0, 0)),
        pl.BlockSpec((1, 1, nk, V_HEAD_DIM), lambda b, h, i: (b, h, 0, 0)),
    ]
    return pl.pallas_call(
        functools.partial(_attn_kernel, tk=tk, n_chunks=n_chunks, rb=rb),
        out_shape=out_shape,
        grid=(bsz, n_heads, n // tq),
        in_specs=in_specs,
        out_specs=out_specs,
        scratch_shapes=[pltpu.VMEM((tq, LANES), F32),
                        pltpu.VMEM((tq, LANES), F32),
                        pltpu.VMEM((tq, V_HEAD_DIM), F32),
                        pltpu.VMEM((2, tq, tk), F32),
                        pltpu.VMEM((2, tq, LANES), F32),
                        pltpu.VMEM((2, tq, tk), BF16),
                        pltpu.VMEM((2, tq, LANES), F32)],
        compiler_params=_params("arbitrary", "arbitrary", "arbitrary"),
        name="mla_attn",
    )(q, k_c, v_c, k_x, v_x)


def _rot_cols(w):
    q = QK_ROPE_DIM // 4
    return jnp.concatenate([-w[..., q:2 * q], w[..., 0:q], -w[..., 3 * q:4 * q], w[..., 2 * q:3 * q]], -1)


def _rope_tables(n):
    rows = n // GRID_W
    n_freq = QK_ROPE_DIM // 4
    inv_freq = 1.0 / (ROPE_THETA ** (jnp.arange(n_freq, dtype=F32) / n_freq))
    row = jnp.broadcast_to(jnp.arange(rows, dtype=F32)[:, None], (rows, GRID_W)).reshape(-1)
    col = jnp.broadcast_to(jnp.arange(GRID_W, dtype=F32)[None, :], (rows, GRID_W)).reshape(-1)
    ang_r = row[:, None] * inv_freq[None, :]
    ang_c = col[:, None] * inv_freq[None, :]
    ang = jnp.concatenate([ang_r, ang_r, ang_c, ang_c], -1)
    return jnp.cos(ang), jnp.sin(ang)


def kernel(x, c, ctx, c_ctx, ada_w, ada_b, ln1_g, ln1_b, ln2_g, ln2_b, sc_w_in, sc_conv_w, sc_w_out, mla_w_dq, mla_q_norm, mla_w_uq, mla_w_dkv, mla_kv_norm, mla_w_uk, mla_w_uv, mla_w_o, ffn_w_up, ffn_conv_w, ffn_conv_b, ffn_w_down):
    bsz, n, d = x.shape
    lc = ctx.shape[1]
    depth = ada_w.shape[0]
    n_heads = N_HEADS
    alpha = (2.0 * depth) ** 0.25

    crow = jnp.concatenate([c, c_ctx[None, :], jnp.zeros((8 - bsz - 1, d), F32)], 0)
    mods = _ada_call(crow, ada_w, ada_b)

    def mod_x(i, k):
        return mods[i, 0:bsz, k * d:(k + 1) * d].reshape(bsz, 1, d)

    def mod_c(i, k):
        return jnp.broadcast_to(mods[i, bsz:bsz + 1, k * d:(k + 1) * d].reshape(1, 1, d), (bsz, 1, d))

    cos, sin = _rope_tables(n)
    zpad = jnp.zeros((n, QK_ROPE_DIM), F32)
    cos128_x = jnp.concatenate([cos, zpad], 1)
    sin128_x = jnp.concatenate([sin, zpad], 1)
    cos128_c = jnp.concatenate([jnp.ones((lc, QK_ROPE_DIM), F32), jnp.zeros((lc, QK_ROPE_DIM), F32)], 1)
    sin128_c = jnp.zeros((lc, 2 * QK_ROPE_DIM), F32)

    for i in range(depth):
        last = i == depth - 1
        j = i // 2
        sh1, sc1, g1, sh2, sc2, g2 = (mod_x(i, k) for k in range(6))
        csh1, csc1, cg1, csh2, csc2, cg2 = (mod_c(i, k) for k in range(6))
        if i % 2 == 0:
            w_in = sc_w_in[j].astype(BF16)
            w_out = sc_w_out[j].astype(BF16)
            zx = _mixer_up_call(x, sc1, sh1, w_in, sc_conv_w[j])
            x1 = _proj_ln_call(zx, w_out, x, g1, ln1_g[i], ln1_b[i], alpha)
            if not last:
                zc = _mixer_up_call(ctx, csc1, csh1, w_in, sc_conv_w[j])
                ctx1 = _proj_ln_call(zc, w_out, ctx, cg1, ln1_g[i], ln1_b[i], alpha)
        else:
            rq = mla_w_dq.shape[-1]
            rkv = mla_kv_norm.shape[-1]
            w_dkv = mla_w_dkv[j]
            w_down = jnp.concatenate([mla_w_dq[j], w_dkv, _rot_cols(w_dkv[:, rkv:])], 1).astype(BF16)
            wq = mla_w_uq[j].reshape(rq, n_heads, QK_HEAD_DIM)
            wq = jnp.concatenate([wq, _rot_cols(wq[..., QK_NOPE_DIM:])], -1)
            wq = wq.reshape(rq, n_heads * (QK_HEAD_DIM + QK_ROPE_DIM)).astype(BF16)
            w_kv = jnp.concatenate([mla_w_uk[j].reshape(rkv, n_heads, QK_NOPE_DIM),
                                    mla_w_uv[j].reshape(rkv, n_heads, V_HEAD_DIM)], -1)
            w_kv = w_kv.reshape(rkv, n_heads * (QK_NOPE_DIM + V_HEAD_DIM)).astype(BF16)
            w_o = mla_w_o[j].astype(BF16)

            ql_x, ckv_x, kr_x = _mla_down_call(x, sc1, sh1, w_down, mla_q_norm[j], mla_kv_norm[j],
                                               cos128_x, sin128_x)
            ql_c, ckv_c, kr_c = _mla_down_call(ctx, csc1, csh1, w_down, mla_q_norm[j], mla_kv_norm[j],
                                               cos128_c, sin128_c)
            k_x, v_x = _kv_up_call(ckv_x, kr_x, w_kv, n_heads)
            k_c, v_c = _kv_up_call(ckv_c, kr_c, w_kv, n_heads)
            q_x = _q_up_call(ql_x, wq, cos128_x, sin128_x, n_heads)
            o_x = _attn_call(q_x, k_c, v_c, k_x, v_x)
            x1 = _proj_ln_call(o_x, w_o, x, g1, ln1_g[i], ln1_b[i], alpha)
            if not last:
                q_c = _q_up_call(ql_c, wq, cos128_c, sin128_c, n_heads)
                o_c = _attn_call(q_c, k_c, v_c)
                ctx1 = _proj_ln_call(o_c, w_o, ctx, cg1, ln1_g[i], ln1_b[i], alpha)

        w_up = ffn_w_up[i].astype(BF16)
        w_dn = ffn_w_down[i].astype(BF16)
        ax = _ffn_up_call(x1, sc2, sh2, w_up, ffn_conv_w[i], ffn_conv_b[i])
        x = _proj_ln_call(ax, w_dn, x1, g2, ln2_g[i], ln2_b[i], alpha)
        if not last:
            ac = _ffn_up_call(ctx1, csc2, csh2, w_up, ffn_conv_w[i], ffn_conv_b[i])
            ctx = _proj_ln_call(ac, w_dn, ctx1, cg2, ln2_g[i], ln2_b[i], alpha)
    return x
```

```python
import functools
import math

import jax
import jax.numpy as jnp
from jax import lax
from jax.experimental import pallas as pl
from jax.experimental.pallas import tpu as pltpu

F32 = jnp.float32
BF16 = jnp.bfloat16

GRID_W = 64
N_HEADS = 16
QK_NOPE_DIM = 128
QK_ROPE_DIM = 64
QK_HEAD_DIM = QK_NOPE_DIM + QK_ROPE_DIM
V_HEAD_DIM = 128
ROPE_THETA = 10000.0
LN_EPS = 1e-5
RMS_EPS = 1e-6

LANES = 128
HALO = 16
ROW_BLOCK = 32
VMEM_LIMIT = 56 * 1024 * 1024


def _params(*sem):
    return pltpu.CompilerParams(dimension_semantics=sem, vmem_limit_bytes=VMEM_LIMIT)


def _ada_kernel(c_ref, w_ref, b_ref, o_ref):
    c = c_ref[...]
    s = c * jax.nn.sigmoid(c)
    o_ref[0] = jnp.dot(s, w_ref[0], precision=lax.Precision.HIGHEST,
                       preferred_element_type=F32) + b_ref[0]


def _ada_call(crow, ada_w, ada_b):
    depth, d, n6 = ada_w.shape
    tn = math.gcd(n6, 1024)
    return pl.pallas_call(
        _ada_kernel,
        out_shape=jax.ShapeDtypeStruct((depth, 8, n6), F32),
        grid=(depth, n6 // tn),
        in_specs=[
            pl.BlockSpec((8, d), lambda l, j: (0, 0)),
            pl.BlockSpec((1, d, tn), lambda l, j: (l, 0, j)),
            pl.BlockSpec((1, 1, tn), lambda l, j: (l, 0, j)),
        ],
        out_specs=pl.BlockSpec((1, 8, tn), lambda l, j: (l, 0, j)),
        compiler_params=_params("arbitrary", "arbitrary"),
        name="ada_mod",
    )(crow, ada_w, ada_b.reshape(depth, 1, n6))


def _fill_lhs(x_ref, xp_ref, xn_ref, sc_ref, sh_ref, lhs_ref, tm):
    i = pl.program_id(1)
    last = pl.num_programs(1) - 1
    scale = 1.0 + sc_ref[0]
    shift = sh_ref[0]
    lhs_ref[0:tm, :] = (x_ref[0] * scale + shift).astype(BF16)
    row = lax.broadcasted_iota(jnp.int32, (HALO, 1), 0)
    hn = xn_ref[0] * scale + shift
    hp = xp_ref[0] * scale + shift
    row_n = jnp.where(i < last, 0, -1)
    row_p = jnp.where(i > 0, HALO - 1, -1)
    halo = jnp.where(row == row_n, hn, jnp.where(row == row_p, hp, 0.0))
    lhs_ref[tm:tm + HALO, :] = halo.astype(BF16)


def _conv3(p, cw, tm):
    rows = p.shape[0]
    prev = pltpu.roll(p, 1, 0)[0:tm]
    nxt = pltpu.roll(p, rows - 1, 0)[0:tm]
    return cw[0:1] * prev + cw[1:2] * p[0:tm] + cw[2:3] * nxt


def _mixer_up_kernel(x_ref, xp_ref, xn_ref, sc_ref, sh_ref, wb_ref, wc_ref, wv_ref,
                     cw_ref, o_ref, lhs_ref, *, tm):
    @pl.when(pl.program_id(2) == 0)
    def _():
        _fill_lhs(x_ref, xp_ref, xn_ref, sc_ref, sh_ref, lhs_ref, tm)

    lhs = lhs_ref[...]
    pb = jnp.dot(lhs_ref[0:tm, :], wb_ref[...], preferred_element_type=F32)
    pc = jnp.dot(lhs, wc_ref[...], preferred_element_type=F32)
    pv = jnp.dot(lhs, wv_ref[...], preferred_element_type=F32)
    o_ref[0] = (pb * _conv3(pc * pv, cw_ref[...], tm)).astype(BF16)


def _ffn_up_kernel(x_ref, xp_ref, xn_ref, sc_ref, sh_ref, wg_ref, wu_ref,
                   cwg_ref, cwu_ref, cbg_ref, cbu_ref, o_ref, lhs_ref, *, tm):
    @pl.when(pl.program_id(2) == 0)
    def _():
        _fill_lhs(x_ref, xp_ref, xn_ref, sc_ref, sh_ref, lhs_ref, tm)

    lhs = lhs_ref[...]
    g = _conv3(jnp.dot(lhs, wg_ref[...], preferred_element_type=F32), cwg_ref[...], tm) + cbg_ref[...]
    u = _conv3(jnp.dot(lhs, wu_ref[...], preferred_element_type=F32), cwu_ref[...], tm) + cbu_ref[...]
    o_ref[0] = (g * jax.nn.sigmoid(g) * u).astype(BF16)


def _halo_specs(n, d, tm):
    r = tm // HALO
    nblk = n // HALO
    return [
        pl.BlockSpec((1, tm, d), lambda b, i, j: (b, i, 0)),
        pl.BlockSpec((1, HALO, d), lambda b, i, j: (b, jnp.maximum(i * r - 1, 0), 0)),
        pl.BlockSpec((1, HALO, d), lambda b, i, j: (b, jnp.minimum((i + 1) * r, nblk - 1), 0)),
        pl.BlockSpec((1, 1, d), lambda b, i, j: (b, 0, 0)),
        pl.BlockSpec((1, 1, d), lambda b, i, j: (b, 0, 0)),
    ]


def _mixer_up_call(x, sc, sh, w_in, conv_w, tm=1024):
    bsz, n, d = x.shape
    tm = min(tm, n)
    fc = math.gcd(d, 512)
    nb = d // fc
    return pl.pallas_call(
        functools.partial(_mixer_up_kernel, tm=tm),
        out_shape=jax.ShapeDtypeStruct((bsz, n, d), BF16),
        grid=(bsz, n // tm, nb),
        in_specs=_halo_specs(n, d, tm) + [
            pl.BlockSpec((d, fc), lambda b, i, j: (0, j)),
            pl.BlockSpec((d, fc), lambda b, i, j: (0, nb + j)),
            pl.BlockSpec((d, fc), lambda b, i, j: (0, 2 * nb + j)),
            pl.BlockSpec((3, fc), lambda b, i, j: (0, j)),
        ],
        out_specs=pl.BlockSpec((1, tm, fc), lambda b, i, j: (b, i, j)),
        scratch_shapes=[pltpu.VMEM((tm + HALO, d), BF16)],
        compiler_params=_params("arbitrary", "arbitrary", "arbitrary"),
        name="mixer_up",
    )(x, x, x, sc, sh, w_in, w_in, w_in, conv_w)


def _ffn_up_call(x, sc, sh, w_up, conv_w, conv_b, tm=1024):
    bsz, n, d = x.shape
    f = w_up.shape[1] // 2
    tm = min(tm, n)
    fc = math.gcd(f, 512)
    nb = f // fc
    conv_b = conv_b.reshape(1, 2 * f)
    return pl.pallas_call(
        functools.partial(_ffn_up_kernel, tm=tm),
        out_shape=jax.ShapeDtypeStruct((bsz, n, f), BF16),
        grid=(bsz, n // tm, nb),
        in_specs=_halo_specs(n, d, tm) + [
            pl.BlockSpec((d, fc), lambda b, i, j: (0, j)),
            pl.BlockSpec((d, fc), lambda b, i, j: (0, nb + j)),
            pl.BlockSpec((3, fc), lambda b, i, j: (0, j)),
            pl.BlockSpec((3, fc), lambda b, i, j: (0, nb + j)),
            pl.BlockSpec((1, fc), lambda b, i, j: (0, j)),
            pl.BlockSpec((1, fc), lambda b, i, j: (0, nb + j)),
        ],
        out_specs=pl.BlockSpec((1, tm, fc), lambda b, i, j: (b, i, j)),
        scratch_shapes=[pltpu.VMEM((tm + HALO, d), BF16)],
        compiler_params=_params("arbitrary", "arbitrary", "arbitrary"),
        name="ffn_up",
    )(x, x, x, sc, sh, w_up, w_up, conv_w, conv_w, conv_b, conv_b)


def _proj_ln_kernel(a_ref, w_ref, x_ref, gate_ref, g_ref, b_ref, o_ref, *, alpha):
    y = jnp.dot(a_ref[0], w_ref[...], preferred_element_type=F32)
    r = alpha * x_ref[0] + gate_ref[0] * y
    mu = jnp.mean(r, axis=-1, keepdims=True)
    rc = r - mu
    var = jnp.mean(rc * rc, axis=-1, keepdims=True)
    o_ref[0] = rc * lax.rsqrt(var + LN_EPS) * g_ref[...] + b_ref[...]


def _proj_ln_call(a, w, x, gate, g, b, alpha, tm=512):
    bsz, n, d = x.shape
    kdim = a.shape[-1]
    tm = min(tm, n)
    return pl.pallas_call(
        functools.partial(_proj_ln_kernel, alpha=alpha),
        out_shape=jax.ShapeDtypeStruct((bsz, n, d), F32),
        grid=(bsz, n // tm),
        in_specs=[
            pl.BlockSpec((1, tm, kdim), lambda b_, i: (b_, i, 0)),
            pl.BlockSpec((kdim, d), lambda b_, i: (0, 0), pipeline_mode=pl.Buffered(1)),
            pl.BlockSpec((1, tm, d), lambda b_, i: (b_, i, 0)),
            pl.BlockSpec((1, 1, d), lambda b_, i: (b_, 0, 0)),
            pl.BlockSpec((1, d), lambda b_, i: (0, 0)),
            pl.BlockSpec((1, d), lambda b_, i: (0, 0)),
        ],
        out_specs=pl.BlockSpec((1, tm, d), lambda b_, i: (b_, i, 0)),
        compiler_params=_params("arbitrary", "arbitrary"),
        name="proj_ln",
    )(a, w, x, gate, g.reshape(1, d), b.reshape(1, d))


def _rms(v, g):
    ms = jnp.mean(v * v, axis=-1, keepdims=True)
    return v * lax.rsqrt(ms + RMS_EPS) * g


def _rope128(t, cos, sin):
    return t * cos + pltpu.roll(t, QK_ROPE_DIM, 1) * sin


def _mla_down_kernel(x_ref, sc_ref, sh_ref, w_ref, qn_ref, kvn_ref, cos_ref, sin_ref,
                     ql_ref, ckv_ref, kr_ref, *, rq, rkv):
    h = (x_ref[0] * (1.0 + sc_ref[0]) + sh_ref[0]).astype(BF16)
    r = jnp.dot(h, w_ref[...], preferred_element_type=F32)
    ql_ref[0] = _rms(r[:, 0:rq], qn_ref[...]).astype(BF16)
    ckv_ref[0] = _rms(r[:, rq:rq + rkv], kvn_ref[...]).astype(BF16)
    t = r[:, rq + rkv:rq + rkv + 2 * QK_ROPE_DIM]
    kr_ref[0] = _rope128(t, cos_ref[...], sin_ref[...]).astype(BF16)


def _mla_down_call(x, sc, sh, w, q_norm, kv_norm, cos128, sin128, tm=1024):
    bsz, n, d = x.shape
    rq, rkv = q_norm.shape[-1], kv_norm.shape[-1]
    wn = w.shape[1]
    tm = min(tm, n)
    return pl.pallas_call(
        functools.partial(_mla_down_kernel, rq=rq, rkv=rkv),
        out_shape=(jax.ShapeDtypeStruct((bsz, n, rq), BF16),
                   jax.ShapeDtypeStruct((bsz, n, rkv), BF16),
                   jax.ShapeDtypeStruct((bsz, n, 2 * QK_ROPE_DIM), BF16)),
        grid=(bsz, n // tm),
        in_specs=[
            pl.BlockSpec((1, tm, d), lambda b, i: (b, i, 0)),
            pl.BlockSpec((1, 1, d), lambda b, i: (b, 0, 0)),
            pl.BlockSpec((1, 1, d), lambda b, i: (b, 0, 0)),
            pl.BlockSpec((d, wn), lambda b, i: (0, 0)),
            pl.BlockSpec((1, rq), lambda b, i: (0, 0)),
            pl.BlockSpec((1, rkv), lambda b, i: (0, 0)),
            pl.BlockSpec((tm, 2 * QK_ROPE_DIM), lambda b, i: (i, 0)),
            pl.BlockSpec((tm, 2 * QK_ROPE_DIM), lambda b, i: (i, 0)),
        ],
        out_specs=(pl.BlockSpec((1, tm, rq), lambda b, i: (b, i, 0)),
                   pl.BlockSpec((1, tm, rkv), lambda b, i: (b, i, 0)),
                   pl.BlockSpec((1, tm, 2 * QK_ROPE_DIM), lambda b, i: (b, i, 0))),
        compiler_params=_params("arbitrary", "arbitrary"),
        name="mla_down",
    )(x, sc, sh, w, q_norm.reshape(1, rq), kv_norm.reshape(1, rkv), cos128, sin128)


def _q_up_kernel(ql_ref, w_ref, cos_ref, sin_ref, o_ref, *, n_heads, qscale):
    ql = ql_ref[0]
    cos = cos_ref[...] * qscale
    sin = sin_ref[...] * qscale
    wh = QK_NOPE_DIM + 2 * QK_ROPE_DIM
    for h in range(n_heads):
        r = jnp.dot(ql, w_ref[:, h * wh:(h + 1) * wh], preferred_element_type=F32)
        o_ref[0, h, :, 0:QK_NOPE_DIM] = (r[:, 0:QK_NOPE_DIM] * qscale).astype(BF16)
        rope = _rope128(r[:, QK_NOPE_DIM:wh], cos, sin)
        o_ref[0, h, :, QK_NOPE_DIM:QK_HEAD_DIM] = rope[:, 0:QK_ROPE_DIM].astype(BF16)


def _q_up_call(ql, wq, cos128, sin128, n_heads, tm=1024):
    bsz, n, rq = ql.shape
    tm = min(tm, n)
    qscale = math.log2(math.e) / math.sqrt(QK_HEAD_DIM)
    return pl.pallas_call(
        functools.partial(_q_up_kernel, n_heads=n_heads, qscale=qscale),
        out_shape=jax.ShapeDtypeStruct((bsz, n_heads, n, QK_HEAD_DIM), BF16),
        grid=(bsz, n // tm),
        in_specs=[
            pl.BlockSpec((1, tm, rq), lambda b, i: (b, i, 0)),
            pl.BlockSpec(wq.shape, lambda b, i: (0, 0)),
            pl.BlockSpec((tm, 2 * QK_ROPE_DIM), lambda b, i: (i, 0)),
            pl.BlockSpec((tm, 2 * QK_ROPE_DIM), lambda b, i: (i, 0)),
        ],
        out_specs=pl.BlockSpec((1, n_heads, tm, QK_HEAD_DIM), lambda b, i: (b, 0, i, 0)),
        compiler_params=_params("arbitrary", "arbitrary"),
        name="q_up",
    )(ql, wq, cos128, sin128)


def _kv_up_kernel(ckv_ref, kr_ref, w_ref, k_ref, v_ref, *, n_heads):
    ckv = ckv_ref[0]
    kr = kr_ref[0][:, 0:QK_ROPE_DIM]
    wh = QK_NOPE_DIM + V_HEAD_DIM
    for h in range(n_heads):
        r = jnp.dot(ckv, w_ref[:, h * wh:(h + 1) * wh], preferred_element_type=F32)
        k_ref[0, h, :, 0:QK_NOPE_DIM] = r[:, 0:QK_NOPE_DIM].astype(BF16)
        k_ref[0, h, :, QK_NOPE_DIM:QK_HEAD_DIM] = kr
        v_ref[0, h] = r[:, QK_NOPE_DIM:wh].astype(BF16)


def _kv_up_call(ckv, kr, w_kv, n_heads, tm=1024):
    bsz, n, rkv = ckv.shape
    tm = min(tm, n)
    return pl.pallas_call(
        functools.partial(_kv_up_kernel, n_heads=n_heads),
        out_shape=(jax.ShapeDtypeStruct((bsz, n_heads, n, QK_HEAD_DIM), BF16),
                   jax.ShapeDtypeStruct((bsz, n_heads, n, V_HEAD_DIM), BF16)),
        grid=(bsz, n // tm),
        in_specs=[
            pl.BlockSpec((1, tm, rkv), lambda b, i: (b, i, 0)),
            pl.BlockSpec((1, tm, 2 * QK_ROPE_DIM), lambda b, i: (b, i, 0)),
            pl.BlockSpec(w_kv.shape, lambda b, i: (0, 0)),
        ],
        out_specs=(pl.BlockSpec((1, n_heads, tm, QK_HEAD_DIM), lambda b, i: (b, 0, i, 0)),
                   pl.BlockSpec((1, n_heads, tm, V_HEAD_DIM), lambda b, i: (b, 0, i, 0))),
        compiler_params=_params("arbitrary", "arbitrary"),
        name="kv_up",
    )(ckv, kr, w_kv)


_NT_DIMS = (((1,), (1,)), ((), ()))


def _tile_lanes(v, width):
    return jnp.concatenate([v] * (width // LANES), axis=1) if width > LANES else v


def _lane_tile_max(s):
    pm = s[:, 0:LANES]
    for c0 in range(LANES, s.shape[1], LANES):
        pm = jnp.maximum(pm, s[:, c0:c0 + LANES])
    return pm


def _ctx_attn_kernel(q_ref, kc_ref, vc_ref, o_ref):
    s = lax.dot_general(q_ref[0, 0], kc_ref[0, 0], _NT_DIMS, preferred_element_type=F32)
    p = jnp.exp2(s - jnp.max(s, axis=1, keepdims=True))
    o = jnp.dot(p.astype(BF16), vc_ref[0, 0], preferred_element_type=F32)
    o_ref[0] = (o / jnp.sum(p, axis=1, keepdims=True)).astype(BF16)


def _attn_kernel(q_ref, kc_ref, vc_ref, k_ref, v_ref, o_ref,
                 m_ref, l_ref, acc_ref, s_ref, pm_ref, p_ref, al_ref, *, tk, n_chunks, rb):
    tq = q_ref.shape[2]

    def scores(j, slot):
        start = pl.multiple_of(j * tk, tk)
        s = lax.dot_general(q_ref[0, 0], k_ref[0, 0, pl.ds(start, tk), :], _NT_DIMS,
                            preferred_element_type=F32)
        s_ref[slot] = s
        pm_ref[slot] = _lane_tile_max(s)

    def softmax(slot):
        m_old = m_ref[...]
        m_new = jnp.maximum(m_old, jnp.max(pm_ref[slot], axis=1, keepdims=True))
        m_ref[...] = m_new
        al_ref[slot] = jnp.exp2(m_old - m_new)
        for r0 in range(0, tq, rb):
            rows = slice(r0, r0 + rb)
            p = jnp.exp2(s_ref[slot, rows, :] - _tile_lanes(m_ref[rows, :], tk))
            p_ref[slot, rows, :] = p.astype(BF16)
            l_ref[rows, :] = al_ref[slot, rows, :] * l_ref[rows, :] + jnp.sum(p, axis=1, keepdims=True)

    def weighted_values(j, slot):
        start = pl.multiple_of(j * tk, tk)
        pv = jnp.dot(p_ref[slot], v_ref[0, 0, pl.ds(start, tk), :], preferred_element_type=F32)
        acc_ref[...] = al_ref[slot] * acc_ref[...] + pv

    s = lax.dot_general(q_ref[0, 0], kc_ref[0, 0], _NT_DIMS, preferred_element_type=F32)
    m0 = jnp.broadcast_to(jnp.max(s, axis=1, keepdims=True), (tq, LANES))
    p = jnp.exp2(s - _tile_lanes(m0, s.shape[1]))
    m_ref[...] = m0
    l_ref[...] = jnp.broadcast_to(jnp.sum(p, axis=1, keepdims=True), (tq, LANES))
    acc_ref[...] = jnp.dot(p.astype(BF16), vc_ref[0, 0], preferred_element_type=F32)

    scores(0, 0)
    softmax(0)
    scores(1, 1)

    def body(i, carry):
        j = 2 * i + 1
        weighted_values(j - 1, 0)
        scores(j + 1, 0)
        softmax(1)
        weighted_values(j, 1)
        scores(j + 2, 1)
        softmax(0)
        return carry

    lax.fori_loop(0, (n_chunks - 2) // 2, body, 0)
    weighted_values(n_chunks - 2, 0)
    softmax(1)
    weighted_values(n_chunks - 1, 1)
    o_ref[0] = (acc_ref[...] / l_ref[...]).astype(BF16)


def _attn_fast_kernel(q_ref, kc_ref, vc_ref, k_ref, v_ref, o_ref, qa_ref, l_ref, acc_ref, s_ref, p_ref,
                      m_ref, pm_ref, al_ref, *, tk, n_chunks, rb):
    tq = q_ref.shape[2]
    q = q_ref[0, 0]
    s = lax.dot_general(q, kc_ref[0, 0], _NT_DIMS, preferred_element_type=F32)
    c = jnp.max(s, axis=1, keepdims=True).astype(BF16).astype(F32)
    p = jnp.exp2(s - c)
    l_ref[...] = jnp.broadcast_to(jnp.sum(p, axis=1, keepdims=True), (tq, LANES))
    acc_ref[...] = jnp.dot(p.astype(BF16), vc_ref[0, 0], preferred_element_type=F32)
    lane_q = lax.broadcasted_iota(jnp.int32, (tq, QK_ROPE_DIM), 1)
    qa_ref[...] = jnp.concatenate([q, jnp.where(lane_q == 0, -c, 0.0).astype(BF16)], axis=1)
    lane_k = lax.broadcasted_iota(jnp.int32, (tk, QK_ROPE_DIM), 1)
    ones_col = jnp.where(lane_k == 0, 1.0, 0.0).astype(BF16)

    def scores(j, slot):
        start = pl.multiple_of(j * tk, tk)
        kj = jnp.concatenate([k_ref[0, 0, pl.ds(start, tk), :], ones_col], axis=1)
        s_ref[slot] = lax.dot_general(qa_ref[...], kj, _NT_DIMS, preferred_element_type=F32)

    def softmax(slot):
        for r0 in range(0, tq, rb):
            rows = slice(r0, r0 + rb)
            p = jnp.exp2(s_ref[slot, rows, :])
            p_ref[slot, rows, :] = p.astype(BF16)
            l_ref[rows, :] = l_ref[rows, :] + jnp.sum(p, axis=1, keepdims=True)

    def weighted_values(j, slot):
        start = pl.multiple_of(j * tk, tk)
        acc_ref[...] += jnp.dot(p_ref[slot], v_ref[0, 0, pl.ds(start, tk), :], preferred_element_type=F32)

    scores(0, 0)
    softmax(0)
    scores(1, 1)

    def body(i, carry):
        j = 2 * i + 1
        weighted_values(j - 1, 0)
        scores(j + 1, 0)
        softmax(1)
        weighted_values(j, 1)
        scores(j + 2, 1)
        softmax(0)
        return carry

    lax.fori_loop(0, (n_chunks - 2) // 2, body, 0)
    weighted_values(n_chunks - 2, 0)
    softmax(1)
    weighted_values(n_chunks - 1, 1)
    o_ref[0] = (acc_ref[...] / l_ref[...]).astype(BF16)

    check = jnp.sum(jnp.abs(acc_ref[...])) + jnp.sum(l_ref[...])
    overflowed = jnp.logical_not(check < jnp.inf)

    @pl.when(overflowed)
    def _():
        _attn_kernel(q_ref, kc_ref, vc_ref, k_ref, v_ref, o_ref, m_ref, l_ref, acc_ref, s_ref,
                     pm_ref, p_ref, al_ref, tk=tk, n_chunks=n_chunks, rb=rb)


def _attn_fast_call(q, k_c, v_c, k_x, v_x, tq=2048, tk=512, rb=ROW_BLOCK):
    bsz, n_heads, n, _ = q.shape
    lc = k_c.shape[2]
    tq = min(tq, n)
    nk = k_x.shape[2]
    tk = min(tk, nk // 2)
    n_chunks = nk // tk
    return pl.pallas_call(
        functools.partial(_attn_fast_kernel, tk=tk, n_chunks=n_chunks, rb=rb),
        out_shape=jax.ShapeDtypeStruct((bsz, n, n_heads * V_HEAD_DIM), BF16),
        grid=(bsz, n_heads, n // tq),
        in_specs=[
            pl.BlockSpec((1, 1, tq, QK_HEAD_DIM), lambda b, h, i: (b, h, i, 0)),
            pl.BlockSpec((1, 1, lc, QK_HEAD_DIM), lambda b, h, i: (b, h, 0, 0)),
            pl.BlockSpec((1, 1, lc, V_HEAD_DIM), lambda b, h, i: (b, h, 0, 0)),
            pl.BlockSpec((1, 1, nk, QK_HEAD_DIM), lambda b, h, i: (b, h, 0, 0)),
            pl.BlockSpec((1, 1, nk, V_HEAD_DIM), lambda b, h, i: (b, h, 0, 0)),
        ],
        out_specs=pl.BlockSpec((1, tq, V_HEAD_DIM), lambda b, h, i: (b, i, h)),
        scratch_shapes=[pltpu.VMEM((tq, QK_HEAD_DIM + QK_ROPE_DIM), BF16),
                        pltpu.VMEM((tq, LANES), F32),
                        pltpu.VMEM((tq, V_HEAD_DIM), F32),
                        pltpu.VMEM((2, tq, tk), F32),
                        pltpu.VMEM((2, tq, tk), BF16),
                        pltpu.VMEM((tq, LANES), F32),
                        pltpu.VMEM((2, tq, LANES), F32),
                        pltpu.VMEM((2, tq, LANES), F32)],
        compiler_params=_params("arbitrary", "arbitrary", "arbitrary"),
        name="mla_attn_fast",
    )(q, k_c, v_c, k_x, v_x)


def _attn_call(q, k_c, v_c, k_x=None, v_x=None, tq=2048, tk=512, rb=ROW_BLOCK):
    bsz, n_heads, n, _ = q.shape
    lc = k_c.shape[2]
    tq = min(tq, n)
    in_specs = [
        pl.BlockSpec((1, 1, tq, QK_HEAD_DIM), lambda b, h, i: (b, h, i, 0)),
        pl.BlockSpec((1, 1, lc, QK_HEAD_DIM), lambda b, h, i: (b, h, 0, 0)),
        pl.BlockSpec((1, 1, lc, V_HEAD_DIM), lambda b, h, i: (b, h, 0, 0)),
    ]
    out_shape = jax.ShapeDtypeStruct((bsz, n, n_heads * V_HEAD_DIM), BF16)
    out_specs = pl.BlockSpec((1, tq, V_HEAD_DIM), lambda b, h, i: (b, i, h))
    if k_x is None:
        return pl.pallas_call(
            _ctx_attn_kernel, out_shape=out_shape, grid=(bsz, n_heads, n // tq),
            in_specs=in_specs, out_specs=out_specs,
            compiler_params=_params("arbitrary", "arbitrary", "arbitrary"),
            name="ctx_attn",
        )(q, k_c, v_c)
    nk = k_x.shape[2]
    tk = min(tk, nk // 2)
    n_chunks = nk // tk
    assert n_chunks % 2 == 0 and n_chunks * tk == nk
    in_specs += [
        pl.BlockSpec((1, 1, nk, QK_HEAD_DIM), lambda b, h, i: (b, h, 0, 0)),
        pl.BlockSpec((1, 1, nk, V_HEAD_DIM), lambda b, h, i: (b, h, 0, 0)),
    ]
    return pl.pallas_call(
        functools.partial(_attn_kernel, tk=tk, n_chunks=n_chunks, rb=rb),
        out_shape=out_shape,
        grid=(bsz, n_heads, n // tq),
        in_specs=in_specs,
        out_specs=out_specs,
        scratch_shapes=[pltpu.VMEM((tq, LANES), F32),
                        pltpu.VMEM((tq, LANES), F32),
                        pltpu.VMEM((tq, V_HEAD_DIM), F32),
                        pltpu.VMEM((2, tq, tk), F32),
                        pltpu.VMEM((2, tq, LANES), F32),
                        pltpu.VMEM((2, tq, tk), BF16),
                        pltpu.VMEM((2, tq, LANES), F32)],
        compiler_params=_params("arbitrary", "arbitrary", "arbitrary"),
        name="mla_attn",
    )(q, k_c, v_c, k_x, v_x)


def _rot_cols(w):
    q = QK_ROPE_DIM // 4
    return jnp.concatenate([-w[..., q:2 * q], w[..., 0:q], -w[..., 3 * q:4 * q], w[..., 2 * q:3 * q]], -1)


def _rope_tables(n):
    rows = n // GRID_W
    n_freq = QK_ROPE_DIM // 4
    inv_freq = 1.0 / (ROPE_THETA ** (jnp.arange(n_freq, dtype=F32) / n_freq))
    row = jnp.broadcast_to(jnp.arange(rows, dtype=F32)[:, None], (rows, GRID_W)).reshape(-1)
    col = jnp.broadcast_to(jnp.arange(GRID_W, dtype=F32)[None, :], (rows, GRID_W)).reshape(-1)
    ang_r = row[:, None] * inv_freq[None, :]
    ang_c = col[:, None] * inv_freq[None, :]
    ang = jnp.concatenate([ang_r, ang_r, ang_c, ang_c], -1)
    return jnp.cos(ang), jnp.sin(ang)


def kernel(x, c, ctx, c_ctx, ada_w, ada_b, ln1_g, ln1_b, ln2_g, ln2_b, sc_w_in, sc_conv_w, sc_w_out, mla_w_dq, mla_q_norm, mla_w_uq, mla_w_dkv, mla_kv_norm, mla_w_uk, mla_w_uv, mla_w_o, ffn_w_up, ffn_conv_w, ffn_conv_b, ffn_w_down):
    bsz, n, d = x.shape
    lc = ctx.shape[1]
    depth = ada_w.shape[0]
    n_heads = N_HEADS
    alpha = (2.0 * depth) ** 0.25

    crow = jnp.concatenate([c, c_ctx[None, :], jnp.zeros((8 - bsz - 1, d), F32)], 0)
    mods = _ada_call(crow, ada_w, ada_b)

    def mod_x(i, k):
        return mods[i, 0:bsz, k * d:(k + 1) * d].reshape(bsz, 1, d)

    def mod_c(i, k):
        return jnp.broadcast_to(mods[i, bsz:bsz + 1, k * d:(k + 1) * d].reshape(1, 1, d), (bsz, 1, d))

    cos, sin = _rope_tables(n)
    zpad = jnp.zeros((n, QK_ROPE_DIM), F32)
    cos128_x = jnp.concatenate([cos, zpad], 1)
    sin128_x = jnp.concatenate([sin, zpad], 1)
    cos128_c = jnp.concatenate([jnp.ones((lc, QK_ROPE_DIM), F32), jnp.zeros((lc, QK_ROPE_DIM), F32)], 1)
    sin128_c = jnp.zeros((lc, 2 * QK_ROPE_DIM), F32)

    for i in range(depth):
        last = i == depth - 1
        j = i // 2
        sh1, sc1, g1, sh2, sc2, g2 = (mod_x(i, k) for k in range(6))
        csh1, csc1, cg1, csh2, csc2, cg2 = (mod_c(i, k) for k in range(6))
        if i % 2 == 0:
            w_in = sc_w_in[j].astype(BF16)
            w_out = sc_w_out[j].astype(BF16)
            zx = _mixer_up_call(x, sc1, sh1, w_in, sc_conv_w[j])
            x1 = _proj_ln_call(zx, w_out, x, g1, ln1_g[i], ln1_b[i], alpha)
            if not last:
                zc = _mixer_up_call(ctx, csc1, csh1, w_in, sc_conv_w[j])
                ctx1 = _proj_ln_call(zc, w_out, ctx, cg1, ln1_g[i], ln1_b[i], alpha)
        else:
            rq = mla_w_dq.shape[-1]
            rkv = mla_kv_norm.shape[-1]
            w_dkv = mla_w_dkv[j]
            w_down = jnp.concatenate([mla_w_dq[j], w_dkv, _rot_cols(w_dkv[:, rkv:])], 1).astype(BF16)
            wq = mla_w_uq[j].reshape(rq, n_heads, QK_HEAD_DIM)
            wq = jnp.concatenate([wq, _rot_cols(wq[..., QK_NOPE_DIM:])], -1)
            wq = wq.reshape(rq, n_heads * (QK_HEAD_DIM + QK_ROPE_DIM)).astype(BF16)
            w_kv = jnp.concatenate([mla_w_uk[j].reshape(rkv, n_heads, QK_NOPE_DIM),
                                    mla_w_uv[j].reshape(rkv, n_heads, V_HEAD_DIM)], -1)
            w_kv = w_kv.reshape(rkv, n_heads * (QK_NOPE_DIM + V_HEAD_DIM)).astype(BF16)
            w_o = mla_w_o[j].astype(BF16)

            ql_x, ckv_x, kr_x = _mla_down_call(x, sc1, sh1, w_down, mla_q_norm[j], mla_kv_norm[j],
                                               cos128_x, sin128_x)
            ql_c, ckv_c, kr_c = _mla_down_call(ctx, csc1, csh1, w_down, mla_q_norm[j], mla_kv_norm[j],
                                               cos128_c, sin128_c)
            k_x, v_x = _kv_up_call(ckv_x, kr_x, w_kv, n_heads)
            k_c, v_c = _kv_up_call(ckv_c, kr_c, w_kv, n_heads)
            q_x = _q_up_call(ql_x, wq, cos128_x, sin128_x, n_heads)
            o_x = _attn_fast_call(q_x, k_c, v_c, k_x, v_x)
            x1 = _proj_ln_call(o_x, w_o, x, g1, ln1_g[i], ln1_b[i], alpha)
            if not last:
                q_c = _q_up_call(ql_c, wq, cos128_c, sin128_c, n_heads)
                o_c = _attn_call(q_c, k_c, v_c)
                ctx1 = _proj_ln_call(o_c, w_o, ctx, cg1, ln1_g[i], ln1_b[i], alpha)

        w_up = ffn_w_up[i].astype(BF16)
        w_dn = ffn_w_down[i].astype(BF16)
        ax = _ffn_up_call(x1, sc2, sh2, w_up, ffn_conv_w[i], ffn_conv_b[i])
        x = _proj_ln_call(ax, w_dn, x1, g2, ln2_g[i], ln2_b[i], alpha)
        if not last:
            ac = _ffn_up_call(ctx1, csc2, csh2, w_up, ffn_conv_w[i], ffn_conv_b[i])
            ctx = _proj_ln_call(ac, w_dn, ctx1, cg2, ln2_g[i], ln2_b[i], alpha)
    return x
```
